```python
import math
import jax, jax.numpy as jnp
from jax import lax
import numpy as np

D_MODEL = 1024
BATCH = 4
SEQ = 4096
DEPTH = 2

CHUNK = 64
N_META = 16
BLOCK_Q = 128
MIX_WIDTH = D_MODEL
A_HEADS = 8
A_HEAD_DIM = (MIX_WIDTH // 2) // A_HEADS
A_WIDTH = A_HEADS * A_HEAD_DIM
B_GROUPS = 8
B_WIDTH = MIX_WIDTH // 2
B_CONV = 3
AB_IN = 3 * A_WIDTH + 3 * B_WIDTH
CF_INNER = D_MODEL
CF_CONV = 31
PEER_HEADS = 8
PEER_KEY_DIM = 256
PEER_HALF = PEER_KEY_DIM // 2
N_KEYS = 128
N_EXPERTS = N_KEYS * N_KEYS
PEER_TOPK = 16
PEER_BLOCK = 128
N_EVEN = (DEPTH + 1) // 2
N_ODD = DEPTH // 2
EPS = 1e-6

kernel_name = "hybrid_stickbreak_shortconv_conformer_peer"


def rmsnorm(x, g):
    xf = x.astype(jnp.float32)
    y = xf * lax.rsqrt(jnp.mean(xf * xf, axis=-1, keepdims=True) + EPS) * g.astype(jnp.float32)
    return y.astype(x.dtype)


def layernorm(x, g, b):
    xf = x.astype(jnp.float32)
    mu = jnp.mean(xf, axis=-1, keepdims=True)
    var = jnp.mean(jnp.square(xf - mu), axis=-1, keepdims=True)
    y = (xf - mu) * lax.rsqrt(var + EPS) * g.astype(jnp.float32) + b.astype(jnp.float32)
    return y.astype(x.dtype)


def causal_depthwise_conv(x, w):
    K, C = w.shape
    return lax.conv_general_dilated(
        x, w[:, None, :].astype(x.dtype), window_strides=(1,), padding=[(K - 1, 0)],
        dimension_numbers=('NWC', 'WIO', 'NWC'), feature_group_count=C)


def stick_breaking_attention(q, k, v):
    B, H, L, dh = q.shape
    nb = L // BLOCK_Q
    qb = q.reshape(B, H, nb, BLOCK_Q, dh).transpose(2, 0, 1, 3, 4)
    key_pos = jnp.arange(L)
    scale = 1.0 / math.sqrt(dh)

    def block(args):
        q_blk, start = args
        z = jnp.einsum('bhqd,bhkd->bhqk', q_blk, k).astype(jnp.float32) * scale
        q_pos = start + jnp.arange(BLOCK_Q)
        mask = key_pos[None, :] < q_pos[:, None]
        log_beta = jax.nn.log_sigmoid(z)
        log_not = jnp.where(mask, jax.nn.log_sigmoid(-z), 0.0)
        log_w = log_beta + lax.cumsum(log_not, axis=3, reverse=True) - log_not
        w = jnp.where(mask, jnp.exp(log_w), 0.0)
        return jnp.einsum('bhqk,bhkd->bhqd', w.astype(v.dtype), v)

    out = lax.map(block, (qb, jnp.arange(nb, dtype=jnp.int32) * BLOCK_Q))
    return out.transpose(1, 2, 0, 3, 4).reshape(B, H, L, dh)


def mixer_ab(y, w_in, conv_w, w_out):
    B, L, _ = y.shape
    proj = y @ w_in
    q, k, v, gate_b, gate_c, h = jnp.split(proj, 6, axis=-1)
    to_heads = lambda t: t.reshape(B, L, A_HEADS, A_HEAD_DIM).transpose(0, 2, 1, 3)
    a_out = stick_breaking_attention(to_heads(q), to_heads(k), to_heads(v))
    a_out = a_out.transpose(0, 2, 1, 3).reshape(B, L, A_WIDTH)
    b_out = gate_b * causal_depthwise_conv(gate_c * h, conv_w)
    return jnp.concatenate([a_out, b_out], axis=-1) @ w_out


def mixer_conformer(y, w_pw1, b_pw1, conv_w, conv_b, ln_g, ln_b, w_pw2, b_pw2):
    z = y @ w_pw1 + b_pw1
    a, g = jnp.split(z, 2, axis=-1)
    z = a * jax.nn.sigmoid(g)
    z = causal_depthwise_conv(z, conv_w) + conv_b
    z = jax.nn.silu(layernorm(z, ln_g, ln_b))
    return z @ w_pw2 + b_pw2


def peer(y, w_q, sub_keys, u, v):
    B, L, D = y.shape
    T = B * L
    yt = y.reshape(T, D)
    q = (yt @ w_q).reshape(T, PEER_HEADS, 2, PEER_HALF)
    s = jnp.einsum('thpc,pnc->thpn', q, sub_keys).astype(jnp.float32)
    val, idx = lax.top_k(s, PEER_TOPK)
    cand = val[:, :, 0, :, None] + val[:, :, 1, None, :]
    cand_idx = idx[:, :, 0, :, None] * N_KEYS + idx[:, :, 1, None, :]
    top_val, top_pos = lax.top_k(cand.reshape(T, PEER_HEADS, PEER_TOPK * PEER_TOPK), PEER_TOPK)
    expert = jnp.take_along_axis(cand_idx.reshape(T, PEER_HEADS, -1), top_pos, axis=-1)
    gate = jax.nn.softmax(top_val, axis=-1).astype(y.dtype)
    nb = T // PEER_BLOCK

    def block(args):
        xb, eb, gb = args
        ub = jnp.take(u, eb, axis=0)
        act = gb * jax.nn.gelu(jnp.einsum('thkd,td->thk', ub, xb), approximate=False)
        vb = jnp.take(v, eb, axis=0)
        return jnp.einsum('thk,thkd->td', act, vb)

    out = lax.map(block, (yt.reshape(nb, PEER_BLOCK, D),
                          expert.reshape(nb, PEER_BLOCK, PEER_HEADS, PEER_TOPK),
                          gate.reshape(nb, PEER_BLOCK, PEER_HEADS, PEER_TOPK)))
    return out.reshape(B, L, D)


def setup_inputs(seed: int = 0) -> dict:
    key = jax.random.key(seed)
    ks = jax.random.split(key, 20)
    n = lambda k, shape, s: jax.random.normal(k, shape, jnp.float32) * s
    D = D_MODEL
    return {
        "x": n(ks[0], (BATCH, SEQ, D), 1.0),
        "meta_tokens": n(ks[1], (N_META, D), 1.0),
        "mix_norm_g": 1.0 + n(ks[2], (DEPTH, D), 0.02),
        "ffn_norm_g": 1.0 + n(ks[3], (DEPTH, D), 0.02),
        "final_norm_g": 1.0 + n(ks[4], (D,), 0.02),
        "ab_w_in": n(ks[5], (N_EVEN, D, AB_IN), D ** -0.5),
        "ab_conv_w": n(ks[6], (N_EVEN, B_CONV, B_WIDTH), B_CONV ** -0.5),
        "ab_w_out": n(ks[7], (N_EVEN, MIX_WIDTH, D), MIX_WIDTH ** -0.5),
        "cf_w_pw1": n(ks[8], (N_ODD, D, 2 * CF_INNER), D ** -0.5),
        "cf_b_pw1": n(ks[9], (N_ODD, 2 * CF_INNER), 0.02),
        "cf_conv_w": n(ks[10], (N_ODD, CF_CONV, CF_INNER), CF_CONV ** -0.5),
        "cf_conv_b": n(ks[11], (N_ODD, CF_INNER), 0.02),
        "cf_ln_g": 1.0 + n(ks[12], (N_ODD, CF_INNER), 0.02),
        "cf_ln_b": n(ks[13], (N_ODD, CF_INNER), 0.02),
        "cf_w_pw2": n(ks[14], (N_ODD, CF_INNER, D), CF_INNER ** -0.5),
        "cf_b_pw2": n(ks[15], (N_ODD, D), 0.02),
        "peer_w_q": n(ks[16], (DEPTH, D, PEER_HEADS * PEER_KEY_DIM), D ** -0.5),
        "peer_sub_keys": n(ks[17], (DEPTH, 2, N_KEYS, PEER_HALF), PEER_HALF ** -0.5),
        "peer_u": n(ks[18], (DEPTH, N_EXPERTS, D), D ** -0.5),
        "peer_v": n(ks[19], (DEPTH, N_EXPERTS, D), PEER_HEADS ** -0.5),
    }


def reference(x, meta_tokens, mix_norm_g, ffn_norm_g, final_norm_g, ab_w_in, ab_conv_w,
              ab_w_out, cf_w_pw1, cf_b_pw1, cf_conv_w, cf_conv_b, cf_ln_g, cf_ln_b,
              cf_w_pw2, cf_b_pw2, peer_w_q, peer_sub_keys, peer_u, peer_v):
    B, S, D = x.shape
    L = S + N_META
    Lp = -(-L // BLOCK_Q) * BLOCK_Q
    h = jnp.concatenate([
        jnp.broadcast_to(meta_tokens.astype(x.dtype)[None], (B, N_META, D)),
        x,
        jnp.zeros((B, Lp - L, D), x.dtype)], axis=1)
    for i in range(DEPTH):
        y = rmsnorm(h, mix_norm_g[i])
        if i % 2 == 0:
            j = i // 2
            h = h + mixer_ab(y, ab_w_in[j], ab_conv_w[j], ab_w_out[j])
        else:
            j = i // 2
            h = h + mixer_conformer(y, cf_w_pw1[j], cf_b_pw1[j], cf_conv_w[j], cf_conv_b[j],
                                    cf_ln_g[j], cf_ln_b[j], cf_w_pw2[j], cf_b_pw2[j])
        y = rmsnorm(h, ffn_norm_g[i])
        h = h + peer(y, peer_w_q[i], peer_sub_keys[i], peer_u[i], peer_v[i])
    h = rmsnorm(h, final_norm_g)
    return h[:, N_META:N_META + S, :]
```

```python
import functools
import math

import jax
import jax.numpy as jnp
from jax import lax
from jax.experimental import pallas as pl
from jax.experimental.pallas import tpu as pltpu

N_META = 16
BLOCK_Q = 128
A_HEADS = 8
A_HEAD_DIM = 64
B_CONV = 3
CF_CONV = 31
PEER_HEADS = 8
N_KEYS = 128
PEER_TOPK = 16
EPS = 1e-6

LANES = 128
SUBLANES = 8
VMEM_LIMIT = 48 * 1024 * 1024

_NT = (((1,), (1,)), ((), ()))


def _cparams(*sem):
    return pltpu.CompilerParams(dimension_semantics=sem, vmem_limit_bytes=VMEM_LIMIT)


def _rms(x, g):
    return x * lax.rsqrt(jnp.mean(x * x, axis=-1, keepdims=True) + EPS) * g


def _inproj_kernel(h_ref, g_ref, w_ref, qkv_ref, bb_ref, *, a3):
    y = _rms(h_ref[...], g_ref[...]).astype(jnp.bfloat16)
    p = jnp.dot(y, w_ref[...], preferred_element_type=jnp.float32)
    qkv_ref[...] = p[:, :a3].astype(jnp.bfloat16)
    bb_ref[...] = p[:, a3:]


def _inproj(h, g, w, a3, tm):
    t, d = h.shape
    n = w.shape[1]
    return pl.pallas_call(
        functools.partial(_inproj_kernel, a3=a3),
        grid=(t // tm,),
        in_specs=[pl.BlockSpec((tm, d), lambda i: (i, 0)),
                  pl.BlockSpec((1, d), lambda i: (0, 0)),
                  pl.BlockSpec((d, n), lambda i: (0, 0))],
        out_specs=[pl.BlockSpec((tm, a3), lambda i: (i, 0)),
                   pl.BlockSpec((tm, n - a3), lambda i: (i, 0))],
        out_shape=[jax.ShapeDtypeStruct((t, a3), jnp.bfloat16),
                   jax.ShapeDtypeStruct((t, n - a3), jnp.float32)],
        compiler_params=_cparams("parallel"),
        name="ab_inproj",
    )(h, g, w)


def _glu_kernel(h_ref, g_ref, w_ref, b_ref, z_ref, *, inner):
    y = _rms(h_ref[...], g_ref[...]).astype(jnp.bfloat16)
    p = jnp.dot(y, w_ref[...], preferred_element_type=jnp.float32) + b_ref[...]
    z_ref[...] = p[:, :inner] * jax.nn.sigmoid(p[:, inner:])


def _glu(h, g, w, b, tm):
    t, d = h.shape
    n = w.shape[1]
    inner = n // 2
    return pl.pallas_call(
        functools.partial(_glu_kernel, inner=inner),
        grid=(t // tm,),
        in_specs=[pl.BlockSpec((tm, d), lambda i: (i, 0)),
                  pl.BlockSpec((1, d), lambda i: (0, 0)),
                  pl.BlockSpec((d, n), lambda i: (0, 0)),
                  pl.BlockSpec((1, n), lambda i: (0, 0))],
        out_specs=pl.BlockSpec((tm, inner), lambda i: (i, 0)),
        out_shape=jax.ShapeDtypeStruct((t, inner), jnp.float32),
        compiler_params=_cparams("parallel"),
        name="cf_glu",
    )(h, g, w, b)


def _attn_kernel(q_ref, k_ref, v_ref, m_ref, o_ref, *, bq):
    i = pl.program_id(2)
    lane = lax.broadcasted_iota(jnp.int32, (bq, LANES), 1)
    head0 = lane < A_HEAD_DIM
    q = q_ref[...]
    zero = jnp.zeros_like(q)
    qs = (jnp.where(head0, q, zero), jnp.where(head0, zero, q))
    mcat = m_ref[...]
    row = lax.broadcasted_iota(jnp.int32, (bq, bq), 0)
    col = lax.broadcasted_iota(jnp.int32, (bq, bq), 1)
    causal = col < row

    def tile(j, carry, masked):
        kj = k_ref[pl.ds(pl.multiple_of(j * bq, bq), bq), :]
        vj = v_ref[pl.ds(pl.multiple_of(j * bq, bq), bq), :]
        new = []
        for hh in range(2):
            c, acc = carry[hh]
            z = lax.dot_general(qs[hh], kj, _NT, preferred_element_type=jnp.float32)
            lb = jnp.minimum(z, 0.0) - jnp.log(1.0 + jnp.exp(-jnp.abs(z)))
            ln = lb - z
            if masked:
                ln = jnp.where(causal, ln, 0.0)
            hi = ln.astype(jnp.bfloat16)
            lo = (ln - hi.astype(jnp.float32)).astype(jnp.bfloat16)
            st = jnp.dot(jnp.concatenate([hi, lo], axis=1), mcat,
                         preferred_element_type=jnp.float32)
            w = jnp.exp(lb + st[:, :bq] + c)
            if masked:
                w = jnp.where(causal, w, 0.0)
            acc = acc + jnp.dot(w.astype(jnp.bfloat16), vj, preferred_element_type=jnp.float32)
            new.append((c + st[:, bq:], acc))
        return tuple(new)

    init = tuple((jnp.zeros((bq, bq), jnp.float32), jnp.zeros((bq, LANES), jnp.float32))
                 for _ in range(2))
    carry = tile(i, init, True)
    carry = lax.fori_loop(0, i, lambda it, cr: tile(i - 1 - it, cr, False), carry)
    o_ref[...] = jnp.where(head0, carry[0][1], carry[1][1]).astype(o_ref.dtype)


def _attention(qkv, mcat, batch, lp, bq):
    t = qkv.shape[0]
    pairs = A_HEADS * A_HEAD_DIM // LANES
    nq = lp // bq
    return pl.pallas_call(
        functools.partial(_attn_kernel, bq=bq),
        grid=(batch, pairs, nq),
        in_specs=[pl.BlockSpec((bq, LANES), lambda b, p, i: (b * nq + i, p)),
                  pl.BlockSpec((lp, LANES), lambda b, p, i: (b, pairs + p)),
                  pl.BlockSpec((lp, LANES), lambda b, p, i: (b, 2 * pairs + p)),
                  pl.BlockSpec((2 * bq, 2 * bq), lambda b, p, i: (0, 0))],
        out_specs=pl.BlockSpec((bq, LANES), lambda b, p, i: (b * nq + i, p)),
        out_shape=jax.ShapeDtypeStruct((t, pairs * LANES), jnp.bfloat16),
        compiler_params=_cparams("parallel", "parallel", "arbitrary"),
        name="sb_attention",
    )(qkv, qkv, qkv, mcat)


def _about_kernel(a_ref, bb_ref, halo_ref, cw_ref, w_ref, h_ref, o_ref, *, nb, bw):
    i = pl.program_id(0)
    bb = bb_ref[...]
    gate_b, gate_c, hb = bb[:, :bw], bb[:, bw:2 * bw], bb[:, 2 * bw:]
    u = gate_c * hb
    hal = halo_ref[...]
    uh = hal[:, bw:2 * bw] * hal[:, 2 * bw:]
    uh = jnp.where(i % nb == 0, 0.0, uh)
    ucat = jnp.concatenate([uh, u], axis=0)
    tm = u.shape[0]
    cw = cw_ref[...]
    conv = u * cw[B_CONV - 1:B_CONV, :]
    for s in range(1, B_CONV):
        us = pltpu.roll(ucat, s, 0)[SUBLANES:SUBLANES + tm, :]
        conv = conv + us * cw[B_CONV - 1 - s:B_CONV - s, :]
    b_out = (gate_b * conv).astype(jnp.bfloat16)
    w = w_ref[...]
    aw = a_ref.shape[1]
    o_ref[...] = (h_ref[...]
                  + jnp.dot(a_ref[...], w[:aw, :], preferred_element_type=jnp.float32)
                  + jnp.dot(b_out, w[aw:, :], preferred_element_type=jnp.float32))


def _about(a_out, bb, conv_w, w_out, h, lp, tm):
    t, d = h.shape
    aw = a_out.shape[1]
    bw = conv_w.shape[1]
    nb = lp // tm
    hb = tm // SUBLANES
    return pl.pallas_call(
        functools.partial(_about_kernel, nb=nb, bw=bw),
        grid=(t // tm,),
        in_specs=[pl.BlockSpec((tm, aw), lambda i: (i, 0)),
                  pl.BlockSpec((tm, 3 * bw), lambda i: (i, 0)),
                  pl.BlockSpec((SUBLANES, 3 * bw), lambda i: (jnp.maximum(i * hb - 1, 0), 0)),
                  pl.BlockSpec((B_CONV, bw), lambda i: (0, 0)),
                  pl.BlockSpec((aw + bw, d), lambda i: (0, 0)),
                  pl.BlockSpec((tm, d), lambda i: (i, 0))],
        out_specs=pl.BlockSpec((tm, d), lambda i: (i, 0)),
        out_shape=jax.ShapeDtypeStruct((t, d), jnp.float32),
        compiler_params=_cparams("parallel"),
        name="ab_out",
    )(a_out, bb, bb, conv_w, w_out, h)


CF_HALO = 32


def _cfout_kernel(z_ref, halo_ref, cw_ref, cb_ref, lg_ref, lb_ref, w_ref, b_ref, h_ref, o_ref,
                  zs_ref, cv_ref, *, nb, rc):
    i = pl.program_id(0)
    tm, c = z_ref.shape
    zs_ref[0:CF_HALO, :] = jnp.where(i % nb == 0, 0.0, halo_ref[...])
    zs_ref[CF_HALO:, :] = z_ref[...]

    def conv_cols(j, _):
        cols = pl.ds(pl.multiple_of(j * LANES, LANES), LANES)
        cw = cw_ref[:, cols]
        cb = cb_ref[:, cols]
        for r in range(0, tm, rc):
            acc = jnp.zeros((rc, LANES), jnp.float32) + cb
            for k in range(CF_CONV):
                off = r + CF_HALO - (CF_CONV - 1) + k
                acc = acc + zs_ref[off:off + rc, cols] * cw[k:k + 1, :]
            cv_ref[r:r + rc, cols] = acc
        return 0

    lax.fori_loop(0, c // LANES, conv_cols, 0)
    acc = cv_ref[...]
    mu = jnp.mean(acc, axis=-1, keepdims=True)
    xc = acc - mu
    var = jnp.mean(xc * xc, axis=-1, keepdims=True)
    y = xc * lax.rsqrt(var + EPS) * lg_ref[...] + lb_ref[...]
    y = y * jax.nn.sigmoid(y)
    o_ref[...] = (h_ref[...] + b_ref[...]
                  + jnp.dot(y.astype(jnp.bfloat16), w_ref[...], preferred_element_type=jnp.float32))


def _cfout(z, conv_w, conv_b, ln_g, ln_b, w2, b2, h, lp, tm):
    t, d = h.shape
    c = z.shape[1]
    nb = lp // tm
    hb = tm // CF_HALO
    vec = lambda n: pl.BlockSpec((1, n), lambda i: (0, 0))
    return pl.pallas_call(
        functools.partial(_cfout_kernel, nb=nb, rc=BLOCK_Q),
        grid=(t // tm,),
        in_specs=[pl.BlockSpec((tm, c), lambda i: (i, 0)),
                  pl.BlockSpec((CF_HALO, c), lambda i: (jnp.maximum(i * hb - 1, 0), 0)),
                  pl.BlockSpec((CF_CONV, c), lambda i: (0, 0)),
                  vec(c), vec(c), vec(c),
                  pl.BlockSpec((c, d), lambda i: (0, 0)),
                  vec(d),
                  pl.BlockSpec((tm, d), lambda i: (i, 0))],
        out_specs=pl.BlockSpec((tm, d), lambda i: (i, 0)),
        out_shape=jax.ShapeDtypeStruct((t, d), jnp.float32),
        scratch_shapes=[pltpu.VMEM((tm + CF_HALO, c), jnp.float32),
                        pltpu.VMEM((tm, c), jnp.float32)],
        compiler_params=_cparams("parallel"),
        name="cf_out",
    )(z, z, conv_w, conv_b, ln_g, ln_b, w2, b2, h)


_NEG = float("-inf")
_BIGKEY = 2 ** 30


def _extract_topk(x, key, k):
    vals, keys = [], []
    for _ in range(k):
        m = jnp.max(x, axis=0, keepdims=True)
        km = jnp.min(jnp.where(x == m, key, _BIGKEY), axis=0, keepdims=True)
        x = jnp.where(key == km, _NEG, x)
        vals.append(m)
        keys.append(km)
    return jnp.concatenate(vals, axis=0), jnp.concatenate(keys, axis=0)


def _route_kernel(h_ref, g_ref, wq_ref, sk_ref, gm_ref, s_ref, val_ref, idx_ref, sel_ref, selt_ref):
    tr = h_ref.shape[0]
    ng = tr // LANES
    nk = N_KEYS
    topk = PEER_TOPK
    nhp = 2 * PEER_HEADS
    y = _rms(h_ref[...], g_ref[...]).astype(jnp.bfloat16)
    q = jnp.dot(y, wq_ref[...], preferred_element_type=jnp.float32).astype(jnp.bfloat16)
    for hp in range(nhp):
        s = lax.dot_general(sk_ref[hp % 2], q[:, hp * nk:(hp + 1) * nk], _NT,
                            preferred_element_type=jnp.float32)
        for c in range(ng):
            s_ref[hp * ng + c] = s[:, c * LANES:(c + 1) * LANES]

    key_iota = lax.broadcasted_iota(jnp.int32, (nk, LANES), 0)

    def level1(n, _):
        v, ix = _extract_topk(s_ref[n], key_iota, topk)
        val_ref[n] = v
        idx_ref[n] = ix
        return 0

    lax.fori_loop(0, nhp * ng, level1, 0)

    half = topk // 2
    sub8 = lax.broadcasted_iota(jnp.int32, (half, LANES), 0)
    sub16 = lax.broadcasted_iota(jnp.int32, (topk, LANES), 0)

    def level2(n, _):
        hd, c = n // ng, n % ng
        n0 = 2 * hd * ng + c
        n1 = n0 + ng
        v0, v1 = val_ref[n0], val_ref[n1]
        i0, i1 = idx_ref[n0], idx_ref[n1]
        code = lambda pos, a, b: pos * (nk * nk) + a * nk + b
        cand = [v0[0:1] + v1]
        keys = [code(sub16, i0[0:1], i1)]
        for k0 in range(1, half):
            cand.append(v0[k0:k0 + 1] + v1[:half])
            keys.append(code(k0 * topk + sub8, i0[k0:k0 + 1], i1[:half]))
        cand.append(v0[half:] + v1[0:1])
        keys.append(code((half + sub8) * topk, i0[half:], i1[0:1]))
        tv, tk = _extract_topk(jnp.concatenate(cand, axis=0), jnp.concatenate(keys, axis=0), topk)
        e = jnp.exp(tv - tv[0:1])
        gate = e / jnp.sum(e, axis=0, keepdims=True)
        expert = tk & (nk * nk - 1)
        rows = pl.ds(pl.multiple_of(hd * topk, topk), topk)
        sel_ref[c, 0, rows, :] = gate
        sel_ref[c, 1, rows, :] = (expert >> 7).astype(jnp.float32)
        sel_ref[c, 2, rows, :] = (expert & (nk - 1)).astype(jnp.float32)
        return 0

    lax.fori_loop(0, PEER_HEADS * ng, level2, 0)

    for c in range(ng):
        for a in range(3):
            selt_ref[a, c * LANES:(c + 1) * LANES, :] = sel_ref[c, a].T

    sub = lax.broadcasted_iota(jnp.int32, (nk, LANES), 0).astype(jnp.float32)
    zblk = jnp.zeros((nk, LANES), jnp.bfloat16)

    def token_pair(tp, _):
        lhs, rhs = [], []
        for s in range(2):
            t = 2 * tp + s
            gt = selt_ref[0, pl.ds(t, 1), :]
            it0 = selt_ref[1, pl.ds(t, 1), :]
            it1 = selt_ref[2, pl.ds(t, 1), :]
            lhs.append(jnp.where(sub == it0, 1.0, 0.0).astype(jnp.bfloat16))
            rhs.append(jnp.where(sub == it1, gt, 0.0).astype(jnp.bfloat16))
        lhs2 = jnp.concatenate(lhs, axis=1)
        rhs2 = jnp.concatenate([jnp.concatenate([rhs[0], zblk], axis=1),
                                jnp.concatenate([zblk, rhs[1]], axis=1)], axis=0)
        g2 = lax.dot_general(lhs2, rhs2, _NT, preferred_element_type=jnp.float32)
        gm_ref[2 * tp] = g2[:, :nk].astype(gm_ref.dtype)
        gm_ref[2 * tp + 1] = g2[:, nk:].astype(gm_ref.dtype)
        return 0

    lax.fori_loop(0, tr // 2, token_pair, 0)


def _route(h, g, wq, sk, tr):
    t, d = h.shape
    nq = wq.shape[1]
    nk = N_KEYS
    ng = tr // LANES
    nhp = 2 * PEER_HEADS
    hk = PEER_HEADS * PEER_TOPK
    return pl.pallas_call(
        _route_kernel,
        grid=(t // tr,),
        in_specs=[pl.BlockSpec((tr, d), lambda i: (i, 0)),
                  pl.BlockSpec((1, d), lambda i: (0, 0)),
                  pl.BlockSpec((d, nq), lambda i: (0, 0)),
                  pl.BlockSpec((2, nk, nk), lambda i: (0, 0, 0))],
        out_specs=pl.BlockSpec((tr, nk, nk), lambda i: (i, 0, 0)),
        out_shape=jax.ShapeDtypeStruct((t, nk, nk), jnp.bfloat16),
        scratch_shapes=[pltpu.VMEM((nhp * ng, nk, LANES), jnp.float32),
                        pltpu.VMEM((nhp * ng, PEER_TOPK, LANES), jnp.float32),
                        pltpu.VMEM((nhp * ng, PEER_TOPK, LANES), jnp.int32),
                        pltpu.VMEM((ng, 3, hk, LANES), jnp.float32),
                        pltpu.VMEM((3, tr, hk), jnp.float32)],
        compiler_params=_cparams("parallel"),
        name="peer_route",
    )(h, g, wq, sk)


def _peer_kernel(h_ref, g_ref, gm_ref, ut_ref, v_ref, fg_ref, o_ref, y_ref, acc_ref, *, final_norm):
    e = pl.program_id(1)

    @pl.when(e == 0)
    def _():
        y_ref[...] = _rms(h_ref[...], g_ref[...]).astype(jnp.bfloat16)
        acc_ref[...] = jnp.zeros_like(acc_ref)

    hid = jnp.dot(y_ref[...], ut_ref[...], preferred_element_type=jnp.float32)
    act = 0.5 * hid * (1.0 + lax.erf(hid * (1.0 / math.sqrt(2.0))))
    a = (gm_ref[...].astype(jnp.float32) * act).astype(jnp.bfloat16)
    acc_ref[...] += jnp.dot(a, v_ref[...], preferred_element_type=jnp.float32)

    @pl.when(e == pl.num_programs(1) - 1)
    def _():
        out = h_ref[...] + acc_ref[...]
        if final_norm:
            out = _rms(out, fg_ref[...])
        o_ref[...] = out


def _peer(h, g, gm, ut, v, fg, tb, eb, final_norm):
    t, d = h.shape
    ne = v.shape[0]
    return pl.pallas_call(
        functools.partial(_peer_kernel, final_norm=final_norm),
        grid=(t // tb, ne // eb),
        in_specs=[pl.BlockSpec((tb, d), lambda i, e: (i, 0)),
                  pl.BlockSpec((1, d), lambda i, e: (0, 0)),
                  pl.BlockSpec((tb, eb), lambda i, e: (i, e)),
                  pl.BlockSpec((d, eb), lambda i, e: (0, e)),
                  pl.BlockSpec((eb, d), lambda i, e: (e, 0)),
                  pl.BlockSpec((1, d), lambda i, e: (0, 0))],
        out_specs=pl.BlockSpec((tb, d), lambda i, e: (i, 0)),
        out_shape=jax.ShapeDtypeStruct((t, d), jnp.float32),
        scratch_shapes=[pltpu.VMEM((tb, d), jnp.bfloat16),
                        pltpu.VMEM((tb, d), jnp.float32)],
        compiler_params=_cparams("parallel", "arbitrary"),
        name="peer_experts",
    )(h, g, gm, ut, v, fg)


def _seq_block(lp):
    return 3 * BLOCK_Q if lp % (3 * BLOCK_Q) == 0 else BLOCK_Q


def _forward(x, meta_tokens, mix_norm_g, ffn_norm_g, final_norm_g, ab_w_in, ab_conv_w,
             ab_w_out, cf_w_pw1, cf_b_pw1, cf_conv_w, cf_conv_b, cf_ln_g, cf_ln_b,
             cf_w_pw2, cf_b_pw2, peer_w_q, peer_sub_keys, peer_u, peer_v,
             route_blk, peer_tb, peer_eb):
    bsz, s, d = x.shape
    l = s + N_META
    lp = -(-l // BLOCK_Q) * BLOCK_Q
    t = bsz * lp
    depth = mix_norm_g.shape[0]
    seq_blk = _seq_block(lp)
    bf = jnp.bfloat16
    row = lambda a: a.reshape(1, -1)

    h = jnp.concatenate([
        jnp.broadcast_to(meta_tokens.astype(x.dtype)[None], (bsz, N_META, d)),
        x,
        jnp.zeros((bsz, lp - l, d), x.dtype)], axis=1).reshape(t, d)

    a_width = A_HEADS * A_HEAD_DIM
    tri = (jnp.arange(BLOCK_Q)[:, None] > jnp.arange(BLOCK_Q)[None, :]).astype(bf)
    mhalf = jnp.concatenate([tri, jnp.ones((BLOCK_Q, BLOCK_Q), bf)], axis=1)
    mcat = jnp.concatenate([mhalf, mhalf], axis=0)
    qscale = jnp.concatenate([jnp.full((a_width,), 1.0 / math.sqrt(A_HEAD_DIM), jnp.float32),
                              jnp.ones((ab_w_in.shape[2] - a_width,), jnp.float32)])

    for i in range(depth):
        j = i // 2
        if i % 2 == 0:
            w_in = (ab_w_in[j] * qscale[None, :]).astype(bf)
            qkv, bb = _inproj(h, row(mix_norm_g[i]), w_in, 3 * a_width, seq_blk)
            a_out = _attention(qkv, mcat, bsz, lp, BLOCK_Q)
            h = _about(a_out, bb, ab_conv_w[j], ab_w_out[j].astype(bf), h, lp, seq_blk)
        else:
            z = _glu(h, row(mix_norm_g[i]), cf_w_pw1[j].astype(bf), row(cf_b_pw1[j]), seq_blk)
            h = _cfout(z, cf_conv_w[j], row(cf_conv_b[j]), row(cf_ln_g[j]), row(cf_ln_b[j]),
                       cf_w_pw2[j].astype(bf), row(cf_b_pw2[j]), h, lp, seq_blk)
        gm = _route(h, row(ffn_norm_g[i]), peer_w_q[i].astype(bf), peer_sub_keys[i].astype(bf),
                    route_blk)
        h = _peer(h, row(ffn_norm_g[i]), gm.reshape(t, N_KEYS * N_KEYS),
                  peer_u[i].astype(bf).T, peer_v[i].astype(bf), row(final_norm_g),
                  peer_tb, peer_eb, final_norm=(i == depth - 1))
    return h.reshape(bsz, lp, d)[:, N_META:N_META + s, :]


def kernel(x, meta_tokens, mix_norm_g, ffn_norm_g, final_norm_g, ab_w_in, ab_conv_w, ab_w_out, cf_w_pw1, cf_b_pw1, cf_conv_w, cf_conv_b, cf_ln_g, cf_ln_b, cf_w_pw2, cf_b_pw2, peer_w_q, peer_sub_keys, peer_u, peer_v):
    return _forward(x, meta_tokens, mix_norm_g, ffn_norm_g, final_norm_g, ab_w_in, ab_conv_w,
                    ab_w_out, cf_w_pw1, cf_b_pw1, cf_conv_w, cf_conv_b, cf_ln_g, cf_ln_b,
                    cf_w_pw2, cf_b_pw2, peer_w_q, peer_sub_keys, peer_u, peer_v,
                    route_blk=256, peer_tb=512, peer_eb=512)
```

```python
import functools
import math

import jax
import jax.numpy as jnp
from jax import lax
from jax.experimental import pallas as pl
from jax.experimental.pallas import tpu as pltpu

N_META = 16
BLOCK_Q = 128
A_HEADS = 8
A_HEAD_DIM = 64
B_CONV = 3
CF_CONV = 31
PEER_HEADS = 8
N_KEYS = 128
PEER_TOPK = 16
EPS = 1e-6

LANES = 128
SUBLANES = 8
VMEM_LIMIT = 48 * 1024 * 1024

_NT = (((1,), (1,)), ((), ()))


def _cparams(*sem):
    return pltpu.CompilerParams(dimension_semantics=sem, vmem_limit_bytes=VMEM_LIMIT)


def _rms(x, g):
    return x * lax.rsqrt(jnp.mean(x * x, axis=-1, keepdims=True) + EPS) * g


def _inproj_kernel(h_ref, g_ref, w_ref, qkv_ref, bb_ref, *, a3):
    y = _rms(h_ref[...], g_ref[...]).astype(jnp.bfloat16)
    p = jnp.dot(y, w_ref[...], preferred_element_type=jnp.float32)
    qkv_ref[...] = p[:, :a3].astype(jnp.bfloat16)
    bb_ref[...] = p[:, a3:]


def _inproj(h, g, w, a3, tm):
    t, d = h.shape
    n = w.shape[1]
    return pl.pallas_call(
        functools.partial(_inproj_kernel, a3=a3),
        grid=(t // tm,),
        in_specs=[pl.BlockSpec((tm, d), lambda i: (i, 0)),
                  pl.BlockSpec((1, d), lambda i: (0, 0)),
                  pl.BlockSpec((d, n), lambda i: (0, 0))],
        out_specs=[pl.BlockSpec((tm, a3), lambda i: (i, 0)),
                   pl.BlockSpec((tm, n - a3), lambda i: (i, 0))],
        out_shape=[jax.ShapeDtypeStruct((t, a3), jnp.bfloat16),
                   jax.ShapeDtypeStruct((t, n - a3), jnp.float32)],
        compiler_params=_cparams("parallel"),
        name="ab_inproj",
    )(h, g, w)


def _glu_kernel(h_ref, g_ref, w_ref, b_ref, z_ref, *, inner):
    y = _rms(h_ref[...], g_ref[...]).astype(jnp.bfloat16)
    p = jnp.dot(y, w_ref[...], preferred_element_type=jnp.float32) + b_ref[...]
    z_ref[...] = p[:, :inner] * jax.nn.sigmoid(p[:, inner:])


def _glu(h, g, w, b, tm):
    t, d = h.shape
    n = w.shape[1]
    inner = n // 2
    return pl.pallas_call(
        functools.partial(_glu_kernel, inner=inner),
        grid=(t // tm,),
        in_specs=[pl.BlockSpec((tm, d), lambda i: (i, 0)),
                  pl.BlockSpec((1, d), lambda i: (0, 0)),
                  pl.BlockSpec((d, n), lambda i: (0, 0)),
                  pl.BlockSpec((1, n), lambda i: (0, 0))],
        out_specs=pl.BlockSpec((tm, inner), lambda i: (i, 0)),
        out_shape=jax.ShapeDtypeStruct((t, inner), jnp.float32),
        compiler_params=_cparams("parallel"),
        name="cf_glu",
    )(h, g, w, b)


F32_EXP_ZERO_BELOW = -104.0


def _attn_kernel(q_ref, k_ref, v_ref, m_ref, o_ref, c_ref, acc_ref, *, bq, pairs):
    i = pl.program_id(1)
    head0 = lax.broadcasted_iota(jnp.int32, (bq, LANES), 1) < A_HEAD_DIM
    row = lax.broadcasted_iota(jnp.int32, (bq, 2 * bq), 0)
    col = lax.broadcasted_iota(jnp.int32, (bq, 2 * bq), 1) & (bq - 1)
    causal = col < row
    mcat = m_ref[...]

    def split_heads(x):
        zero = jnp.zeros_like(x)
        return jnp.concatenate([jnp.where(head0, x, zero), jnp.where(head0, zero, x)], axis=0)

    def tile(j, first):
        rows = pl.ds(pl.multiple_of(j * bq, bq), bq)
        cmax = None
        for p in range(pairs):
            lanes = slice(p * LANES, (p + 1) * LANES)
            z = lax.dot_general(q_ref[:, lanes], split_heads(k_ref[rows, lanes]), _NT,
                                preferred_element_type=jnp.float32)
            lb = jnp.minimum(z, 0.0) - jnp.log(1.0 + jnp.exp(-jnp.abs(z)))
            ln = lb - z
            if first:
                ln = jnp.where(causal, ln, 0.0)
            hi = ln.astype(jnp.bfloat16)
            lo = (ln - hi.astype(jnp.float32)).astype(jnp.bfloat16)
            st = [jnp.dot(jnp.concatenate([hi[:, h * bq:(h + 1) * bq], lo[:, h * bq:(h + 1) * bq]], axis=1),
                          mcat, preferred_element_type=jnp.float32) for h in range(2)]
            suffix = jnp.concatenate([st[0][:, :bq], st[1][:, :bq]], axis=1)
            total = jnp.concatenate([st[0][:, bq:], st[1][:, bq:]], axis=1)
            if first:
                w = jnp.where(causal, jnp.exp(lb + suffix), 0.0)
                c_new = total
            else:
                c = c_ref[p]
                w = jnp.exp(lb + suffix + c)
                c_new = c + total
            pv = jnp.dot(w.astype(jnp.bfloat16), split_heads(v_ref[rows, lanes]),
                         preferred_element_type=jnp.float32)
            acc_ref[p] = pv if first else acc_ref[p] + pv
            c_ref[p] = c_new
            cm = jnp.maximum(c_new[:, :bq], c_new[:, bq:])
            cmax = cm if cmax is None else jnp.maximum(cmax, cm)
        return jnp.max(cmax)

    cm0 = tile(i, True)

    def cond(st):
        j, cm = st
        return jnp.logical_and(j >= 0, cm >= F32_EXP_ZERO_BELOW)

    def body(st):
        j, _ = st
        return j - 1, tile(j, False)

    lax.while_loop(cond, body, (i - 1, cm0))
    o_ref[...] = jnp.concatenate([acc_ref[p] for p in range(pairs)], axis=1).astype(o_ref.dtype)


def _attention(qkv, mcat, batch, lp, bq):
    t = qkv.shape[0]
    aw = A_HEADS * A_HEAD_DIM
    pairs = aw // LANES
    nq = lp // bq
    return pl.pallas_call(
        functools.partial(_attn_kernel, bq=bq, pairs=pairs),
        grid=(batch, nq),
        in_specs=[pl.BlockSpec((bq, aw), lambda b, i: (b * nq + i, 0)),
                  pl.BlockSpec((lp, aw), lambda b, i: (b, 1)),
                  pl.BlockSpec((lp, aw), lambda b, i: (b, 2)),
                  pl.BlockSpec((2 * bq, 2 * bq), lambda b, i: (0, 0))],
        out_specs=pl.BlockSpec((bq, aw), lambda b, i: (b * nq + i, 0)),
        out_shape=jax.ShapeDtypeStruct((t, aw), jnp.bfloat16),
        scratch_shapes=[pltpu.VMEM((pairs, bq, 2 * bq), jnp.float32),
                        pltpu.VMEM((pairs, bq, LANES), jnp.float32)],
        compiler_params=_cparams("parallel", "arbitrary"),
        name="sb_attention",
    )(qkv, qkv, qkv, mcat)


def _about_kernel(a_ref, bb_ref, halo_ref, cw_ref, w_ref, h_ref, o_ref, *, nb, bw):
    i = pl.program_id(0)
    bb = bb_ref[...]
    gate_b, gate_c, hb = bb[:, :bw], bb[:, bw:2 * bw], bb[:, 2 * bw:]
    u = gate_c * hb
    hal = halo_ref[...]
    uh = hal[:, bw:2 * bw] * hal[:, 2 * bw:]
    uh = jnp.where(i % nb == 0, 0.0, uh)
    ucat = jnp.concatenate([uh, u], axis=0)
    tm = u.shape[0]
    cw = cw_ref[...]
    conv = u * cw[B_CONV - 1:B_CONV, :]
    for s in range(1, B_CONV):
        us = pltpu.roll(ucat, s, 0)[SUBLANES:SUBLANES + tm, :]
        conv = conv + us * cw[B_CONV - 1 - s:B_CONV - s, :]
    b_out = (gate_b * conv).astype(jnp.bfloat16)
    w = w_ref[...]
    aw = a_ref.shape[1]
    o_ref[...] = (h_ref[...]
                  + jnp.dot(a_ref[...], w[:aw, :], preferred_element_type=jnp.float32)
                  + jnp.dot(b_out, w[aw:, :], preferred_element_type=jnp.float32))


def _about(a_out, bb, conv_w, w_out, h, lp, tm):
    t, d = h.shape
    aw = a_out.shape[1]
    bw = conv_w.shape[1]
    nb = lp // tm
    hb = tm // SUBLANES
    return pl.pallas_call(
        functools.partial(_about_kernel, nb=nb, bw=bw),
        grid=(t // tm,),
        in_specs=[pl.BlockSpec((tm, aw), lambda i: (i, 0)),
                  pl.BlockSpec((tm, 3 * bw), lambda i: (i, 0)),
                  pl.BlockSpec((SUBLANES, 3 * bw), lambda i: (jnp.maximum(i * hb - 1, 0), 0)),
                  pl.BlockSpec((B_CONV, bw), lambda i: (0, 0)),
                  pl.BlockSpec((aw + bw, d), lambda i: (0, 0)),
                  pl.BlockSpec((tm, d), lambda i: (i, 0))],
        out_specs=pl.BlockSpec((tm, d), lambda i: (i, 0)),
        out_shape=jax.ShapeDtypeStruct((t, d), jnp.float32),
        compiler_params=_cparams("parallel"),
        name="ab_out",
    )(a_out, bb, bb, conv_w, w_out, h)


CF_HALO = 32


def _cfout_kernel(z_ref, halo_ref, cw_ref, cb_ref, lg_ref, lb_ref, w_ref, b_ref, h_ref, o_ref,
                  zs_ref, cv_ref, *, nb, rc):
    i = pl.program_id(0)
    tm, c = z_ref.shape
    zs_ref[0:CF_HALO, :] = jnp.where(i % nb == 0, 0.0, halo_ref[...])
    zs_ref[CF_HALO:, :] = z_ref[...]

    def conv_cols(j, _):
        cols = pl.ds(pl.multiple_of(j * LANES, LANES), LANES)
        cw = cw_ref[:, cols]
        cb = cb_ref[:, cols]
        for r in range(0, tm, rc):
            acc = jnp.zeros((rc, LANES), jnp.float32) + cb
            for k in range(CF_CONV):
                off = r + CF_HALO - (CF_CONV - 1) + k
                acc = acc + zs_ref[off:off + rc, cols] * cw[k:k + 1, :]
            cv_ref[r:r + rc, cols] = acc
        return 0

    lax.fori_loop(0, c // LANES, conv_cols, 0)
    acc = cv_ref[...]
    mu = jnp.mean(acc, axis=-1, keepdims=True)
    xc = acc - mu
    var = jnp.mean(xc * xc, axis=-1, keepdims=True)
    y = xc * lax.rsqrt(var + EPS) * lg_ref[...] + lb_ref[...]
    y = y * jax.nn.sigmoid(y)
    o_ref[...] = (h_ref[...] + b_ref[...]
                  + jnp.dot(y.astype(jnp.bfloat16), w_ref[...], preferred_element_type=jnp.float32))


def _cfout(z, conv_w, conv_b, ln_g, ln_b, w2, b2, h, lp, tm):
    t, d = h.shape
    c = z.shape[1]
    nb = lp // tm
    hb = tm // CF_HALO
    vec = lambda n: pl.BlockSpec((1, n), lambda i: (0, 0))
    return pl.pallas_call(
        functools.partial(_cfout_kernel, nb=nb, rc=BLOCK_Q),
        grid=(t // tm,),
        in_specs=[pl.BlockSpec((tm, c), lambda i: (i, 0)),
                  pl.BlockSpec((CF_HALO, c), lambda i: (jnp.maximum(i * hb - 1, 0), 0)),
                  pl.BlockSpec((CF_CONV, c), lambda i: (0, 0)),
                  vec(c), vec(c), vec(c),
                  pl.BlockSpec((c, d), lambda i: (0, 0)),
                  vec(d),
                  pl.BlockSpec((tm, d), lambda i: (i, 0))],
        out_specs=pl.BlockSpec((tm, d), lambda i: (i, 0)),
        out_shape=jax.ShapeDtypeStruct((t, d), jnp.float32),
        scratch_shapes=[pltpu.VMEM((tm + CF_HALO, c), jnp.float32),
                        pltpu.VMEM((tm, c), jnp.float32)],
        compiler_params=_cparams("parallel"),
        name="cf_out",
    )(z, z, conv_w, conv_b, ln_g, ln_b, w2, b2, h)


_NEG = float("-inf")
_BIGKEY = float(2 ** 30)
G_ROW_STRIDE = N_KEYS + SUBLANES


def _extract_topk(x, key, k):
    vals, keys = [], []
    for _ in range(k):
        m = jnp.max(x, axis=0, keepdims=True)
        km = jnp.min(jnp.where(x == m, key, _BIGKEY), axis=0, keepdims=True)
        x = jnp.where(key == km, _NEG, x)
        vals.append(m)
        keys.append(km)
    return jnp.concatenate(vals, axis=0), jnp.concatenate(keys, axis=0)


def _route_kernel(h_ref, g_ref, wq_ref, sk_ref, gm_ref, s_ref, val_ref, idx_ref, sel_ref, selt_ref,
                  gs_ref):
    tr = h_ref.shape[0]
    nk = N_KEYS
    topk = PEER_TOPK
    nhp = 2 * PEER_HEADS
    y = _rms(h_ref[...], g_ref[...]).astype(jnp.bfloat16)
    q = jnp.dot(y, wq_ref[...], preferred_element_type=jnp.float32).astype(jnp.bfloat16)
    for hp in range(nhp):
        s_ref[hp] = lax.dot_general(sk_ref[hp % 2], q[:, hp * nk:(hp + 1) * nk], _NT,
                                    preferred_element_type=jnp.float32)

    key_iota = lax.broadcasted_iota(jnp.int32, (nk, tr), 0).astype(jnp.float32)

    def level1(n, _):
        for s in range(2):
            v, ix = _extract_topk(s_ref[2 * n + s], key_iota, topk)
            val_ref[2 * n + s] = v
            idx_ref[2 * n + s] = ix
        return 0

    lax.fori_loop(0, nhp // 2, level1, 0)

    half = topk // 2
    sub8 = lax.broadcasted_iota(jnp.int32, (half, tr), 0).astype(jnp.float32)
    sub16 = lax.broadcasted_iota(jnp.int32, (topk, tr), 0).astype(jnp.float32)

    def level2(n, _):
        for s in range(2):
            hd = 2 * n + s
            v0, v1 = val_ref[2 * hd], val_ref[2 * hd + 1]
            i0, i1 = idx_ref[2 * hd], idx_ref[2 * hd + 1]
            code = lambda pos, a, b: pos * float(nk * nk) + a * float(nk) + b
            cand = [v0[0:1] + v1]
            keys = [code(sub16, i0[0:1], i1)]
            for k0 in range(1, half):
                cand.append(v0[k0:k0 + 1] + v1[:half])
                keys.append(code(k0 * topk + sub8, i0[k0:k0 + 1], i1[:half]))
            cand.append(v0[half:] + v1[0:1])
            keys.append(code((half + sub8) * topk, i0[half:], i1[0:1]))
            tv, tk = _extract_topk(jnp.concatenate(cand, axis=0), jnp.concatenate(keys, axis=0), topk)
            e = jnp.exp(tv - tv[0:1])
            gate = e / jnp.sum(e, axis=0, keepdims=True)
            expert = tk.astype(jnp.int32) & (nk * nk - 1)
            rows = pl.ds(pl.multiple_of(hd * topk, topk), topk)
            sel_ref[0, rows, :] = gate
            sel_ref[1, rows, :] = (expert >> 7).astype(jnp.float32)
            sel_ref[2, rows, :] = (expert & (nk - 1)).astype(jnp.float32)
        return 0

    lax.fori_loop(0, PEER_HEADS // 2, level2, 0)

    for a in range(3):
        selt_ref[a] = sel_ref[a].T

    sub = lax.broadcasted_iota(jnp.int32, (nk, LANES), 0).astype(jnp.float32)
    zblk = jnp.zeros((nk, LANES), jnp.bfloat16)

    def token_pair(tp, _):
        lhs, rhs = [], []
        for s in range(2):
            t = 2 * tp + s
            gt = selt_ref[0, pl.ds(t, 1), :]
            it0 = selt_ref[1, pl.ds(t, 1), :]
            it1 = selt_ref[2, pl.ds(t, 1), :]
            lhs.append(jnp.where(sub == it0, 1.0, 0.0).astype(jnp.bfloat16))
            rhs.append(jnp.where(sub == it1, gt, 0.0).astype(jnp.bfloat16))
        lhs2 = jnp.concatenate(lhs, axis=1)
        rhs2 = jnp.concatenate([jnp.concatenate([rhs[0], zblk], axis=1),
                                jnp.concatenate([zblk, rhs[1]], axis=1)], axis=0)
        g2 = lax.dot_general(lhs2, rhs2, _NT, preferred_element_type=jnp.float32)
        for s in range(2):
            base = pl.multiple_of((2 * tp + s) * G_ROW_STRIDE, SUBLANES)
            gs_ref[pl.ds(base, nk), :] = g2[:, s * nk:(s + 1) * nk]
        return 0

    lax.fori_loop(0, tr // 2, token_pair, 0, unroll=4)

    for i in range(nk):
        gm_ref[:, i * nk:(i + 1) * nk] = gs_ref[pl.ds(i, tr, stride=G_ROW_STRIDE), :].astype(gm_ref.dtype)


def _route(h, g, wq, sk):
    t, d = h.shape
    nq = wq.shape[1]
    nk = N_KEYS
    tr = LANES
    nhp = 2 * PEER_HEADS
    hk = PEER_HEADS * PEER_TOPK
    return pl.pallas_call(
        _route_kernel,
        grid=(t // tr,),
        in_specs=[pl.BlockSpec((tr, d), lambda i: (i, 0)),
                  pl.BlockSpec((1, d), lambda i: (0, 0)),
                  pl.BlockSpec((d, nq), lambda i: (0, 0)),
                  pl.BlockSpec((2, nk, nk), lambda i: (0, 0, 0))],
        out_specs=pl.BlockSpec((tr, nk * nk), lambda i: (i, 0)),
        out_shape=jax.ShapeDtypeStruct((t, nk * nk), jnp.bfloat16),
        scratch_shapes=[pltpu.VMEM((nhp, nk, tr), jnp.float32),
                        pltpu.VMEM((nhp, PEER_TOPK, tr), jnp.float32),
                        pltpu.VMEM((nhp, PEER_TOPK, tr), jnp.float32),
                        pltpu.VMEM((3, hk, tr), jnp.float32),
                        pltpu.VMEM((3, tr, hk), jnp.float32),
                        pltpu.VMEM((tr * G_ROW_STRIDE, nk), jnp.float32)],
        compiler_params=_cparams("parallel"),
        name="peer_route",
    )(h, g, wq, sk)


def _peer_kernel(h_ref, g_ref, gm_ref, ut_ref, v_ref, fg_ref, o_ref, y_ref, acc_ref, *, final_norm):
    e = pl.program_id(1)

    @pl.when(e == 0)
    def _():
        y_ref[...] = _rms(h_ref[...], g_ref[...]).astype(jnp.bfloat16)
        acc_ref[...] = jnp.zeros_like(acc_ref)

    hid = jnp.dot(y_ref[...], ut_ref[...], preferred_element_type=jnp.float32)
    act = 0.5 * hid * (1.0 + lax.erf(hid * (1.0 / math.sqrt(2.0))))
    a = (gm_ref[...].astype(jnp.float32) * act).astype(jnp.bfloat16)
    acc_ref[...] += jnp.dot(a, v_ref[...], preferred_element_type=jnp.float32)

    @pl.when(e == pl.num_programs(1) - 1)
    def _():
        out = h_ref[...] + acc_ref[...]
        if final_norm:
            out = _rms(out, fg_ref[...])
        o_ref[...] = out


def _peer(h, g, gm, ut, v, fg, tb, eb, final_norm):
    t, d = h.shape
    ne = v.shape[0]
    return pl.pallas_call(
        functools.partial(_peer_kernel, final_norm=final_norm),
        grid=(t // tb, ne // eb),
        in_specs=[pl.BlockSpec((tb, d), lambda i, e: (i, 0)),
                  pl.BlockSpec((1, d), lambda i, e: (0, 0)),
                  pl.BlockSpec((tb, eb), lambda i, e: (i, e)),
                  pl.BlockSpec((d, eb), lambda i, e: (0, e)),
                  pl.BlockSpec((eb, d), lambda i, e: (e, 0)),
                  pl.BlockSpec((1, d), lambda i, e: (0, 0))],
        out_specs=pl.BlockSpec((tb, d), lambda i, e: (i, 0)),
        out_shape=jax.ShapeDtypeStruct((t, d), jnp.float32),
        scratch_shapes=[pltpu.VMEM((tb, d), jnp.bfloat16),
                        pltpu.VMEM((tb, d), jnp.float32)],
        compiler_params=_cparams("parallel", "arbitrary"),
        name="peer_experts",
    )(h, g, gm, ut, v, fg)


def _seq_block(lp):
    return 3 * BLOCK_Q if lp % (3 * BLOCK_Q) == 0 else BLOCK_Q


def _forward(x, meta_tokens, mix_norm_g, ffn_norm_g, final_norm_g, ab_w_in, ab_conv_w,
             ab_w_out, cf_w_pw1, cf_b_pw1, cf_conv_w, cf_conv_b, cf_ln_g, cf_ln_b,
             cf_w_pw2, cf_b_pw2, peer_w_q, peer_sub_keys, peer_u, peer_v,
             peer_tb, peer_eb):
    bsz, s, d = x.shape
    l = s + N_META
    lp = -(-l // BLOCK_Q) * BLOCK_Q
    t = bsz * lp
    depth = mix_norm_g.shape[0]
    seq_blk = _seq_block(lp)
    bf = jnp.bfloat16
    row = lambda a: a.reshape(1, -1)

    h = jnp.concatenate([
        jnp.broadcast_to(meta_tokens.astype(x.dtype)[None], (bsz, N_META, d)),
        x,
        jnp.zeros((bsz, lp - l, d), x.dtype)], axis=1).reshape(t, d)

    a_width = A_HEADS * A_HEAD_DIM
    tri = (jnp.arange(BLOCK_Q)[:, None] > jnp.arange(BLOCK_Q)[None, :]).astype(bf)
    mhalf = jnp.concatenate([tri, jnp.ones((BLOCK_Q, BLOCK_Q), bf)], axis=1)
    mcat = jnp.concatenate([mhalf, mhalf], axis=0)
    qscale = jnp.concatenate([jnp.full((a_width,), 1.0 / math.sqrt(A_HEAD_DIM), jnp.float32),
                              jnp.ones((ab_w_in.shape[2] - a_width,), jnp.float32)])

    for i in range(depth):
        j = i // 2
        if i % 2 == 0:
            w_in = (ab_w_in[j] * qscale[None, :]).astype(bf)
            qkv, bb = _inproj(h, row(mix_norm_g[i]), w_in, 3 * a_width, seq_blk)
            a_out = _attention(qkv, mcat, bsz, lp, BLOCK_Q)
            h = _about(a_out, bb, ab_conv_w[j], ab_w_out[j].astype(bf), h, lp, seq_blk)
        else:
            z = _glu(h, row(mix_norm_g[i]), cf_w_pw1[j].astype(bf), row(cf_b_pw1[j]), seq_blk)
            h = _cfout(z, cf_conv_w[j], row(cf_conv_b[j]), row(cf_ln_g[j]), row(cf_ln_b[j]),
                       cf_w_pw2[j].astype(bf), row(cf_b_pw2[j]), h, lp, seq_blk)
        gm = _route(h, row(ffn_norm_g[i]), peer_w_q[i].astype(bf), peer_sub_keys[i].astype(bf))
        h = _peer(h, row(ffn_norm_g[i]), gm,
                  peer_u[i].astype(bf).T, peer_v[i].astype(bf), row(final_norm_g),
                  peer_tb, peer_eb, final_norm=(i == depth - 1))
    return h.reshape(bsz, lp, d)[:, N_META:N_META + s, :]


def kernel(x, meta_tokens, mix_norm_g, ffn_norm_g, final_norm_g, ab_w_in, ab_conv_w, ab_w_out, cf_w_pw1, cf_b_pw1, cf_conv_w, cf_conv_b, cf_ln_g, cf_ln_b, cf_w_pw2, cf_b_pw2, peer_w_q, peer_sub_keys, peer_u, peer_v):
    return _forward(x, meta_tokens, mix_norm_g, ffn_norm_g, final_norm_g, ab_w_in, ab_conv_w,
                    ab_w_out, cf_w_pw1, cf_b_pw1, cf_conv_w, cf_conv_b, cf_ln_g, cf_ln_b,
                    cf_w_pw2, cf_b_pw2, peer_w_q, peer_sub_keys, peer_u, peer_v,
                    peer_tb=512, peer_eb=512)
```

```python
import functools
import math

import jax
import jax.numpy as jnp
from jax import lax
from jax.experimental import pallas as pl
from jax.experimental.pallas import tpu as pltpu

N_META = 16
BLOCK_Q = 128
A_HEADS = 8
A_HEAD_DIM = 64
B_CONV = 3
CF_CONV = 31
PEER_HEADS = 8
N_KEYS = 128
PEER_TOPK = 16
EPS = 1e-6

LANES = 128
SUBLANES = 8
VMEM_LIMIT = 48 * 1024 * 1024

_NT = (((1,), (1,)), ((), ()))


def _cparams(*sem):
    return pltpu.CompilerParams(dimension_semantics=sem, vmem_limit_bytes=VMEM_LIMIT)


def _rms(x, g):
    return x * lax.rsqrt(jnp.mean(x * x, axis=-1, keepdims=True) + EPS) * g


def _inproj_kernel(h_ref, g_ref, w_ref, qkv_ref, bb_ref, *, a3):
    y = _rms(h_ref[...], g_ref[...]).astype(jnp.bfloat16)
    p = jnp.dot(y, w_ref[...], preferred_element_type=jnp.float32)
    qkv_ref[...] = p[:, :a3].astype(jnp.bfloat16)
    bb_ref[...] = p[:, a3:]


def _inproj(h, g, w, a3, tm):
    t, d = h.shape
    n = w.shape[1]
    return pl.pallas_call(
        functools.partial(_inproj_kernel, a3=a3),
        grid=(t // tm,),
        in_specs=[pl.BlockSpec((tm, d), lambda i: (i, 0)),
                  pl.BlockSpec((1, d), lambda i: (0, 0)),
                  pl.BlockSpec((d, n), lambda i: (0, 0))],
        out_specs=[pl.BlockSpec((tm, a3), lambda i: (i, 0)),
                   pl.BlockSpec((tm, n - a3), lambda i: (i, 0))],
        out_shape=[jax.ShapeDtypeStruct((t, a3), jnp.bfloat16),
                   jax.ShapeDtypeStruct((t, n - a3), jnp.float32)],
        compiler_params=_cparams("parallel"),
        name="ab_inproj",
    )(h, g, w)


def _glu_kernel(h_ref, g_ref, w_ref, b_ref, z_ref, *, inner):
    y = _rms(h_ref[...], g_ref[...]).astype(jnp.bfloat16)
    p = jnp.dot(y, w_ref[...], preferred_element_type=jnp.float32) + b_ref[...]
    z_ref[...] = p[:, :inner] * jax.nn.sigmoid(p[:, inner:])


def _glu(h, g, w, b, tm):
    t, d = h.shape
    n = w.shape[1]
    inner = n // 2
    return pl.pallas_call(
        functools.partial(_glu_kernel, inner=inner),
        grid=(t // tm,),
        in_specs=[pl.BlockSpec((tm, d), lambda i: (i, 0)),
                  pl.BlockSpec((1, d), lambda i: (0, 0)),
                  pl.BlockSpec((d, n), lambda i: (0, 0)),
                  pl.BlockSpec((1, n), lambda i: (0, 0))],
        out_specs=pl.BlockSpec((tm, inner), lambda i: (i, 0)),
        out_shape=jax.ShapeDtypeStruct((t, inner), jnp.float32),
        compiler_params=_cparams("parallel"),
        name="cf_glu",
    )(h, g, w, b)


F32_EXP_ZERO_BELOW = -104.0


def _attn_kernel(q_ref, k_ref, v_ref, m_ref, o_ref, c_ref, acc_ref, *, bq, pairs):
    i = pl.program_id(1)
    head0 = lax.broadcasted_iota(jnp.int32, (bq, LANES), 1) < A_HEAD_DIM
    row = lax.broadcasted_iota(jnp.int32, (bq, 2 * bq), 0)
    col = lax.broadcasted_iota(jnp.int32, (bq, 2 * bq), 1) & (bq - 1)
    causal = col < row
    mcat = m_ref[...]

    def split_heads(x):
        zero = jnp.zeros_like(x)
        return jnp.concatenate([jnp.where(head0, x, zero), jnp.where(head0, zero, x)], axis=0)

    def tile(j, first):
        rows = pl.ds(pl.multiple_of(j * bq, bq), bq)
        cmax = None
        for p in range(pairs):
            lanes = slice(p * LANES, (p + 1) * LANES)
            z = lax.dot_general(q_ref[:, lanes], split_heads(k_ref[rows, lanes]), _NT,
                                preferred_element_type=jnp.float32)
            lb = jnp.minimum(z, 0.0) - jnp.log(1.0 + jnp.exp(-jnp.abs(z)))
            ln = lb - z
            if first:
                ln = jnp.where(causal, ln, 0.0)
            hi = ln.astype(jnp.bfloat16)
            lo = (ln - hi.astype(jnp.float32)).astype(jnp.bfloat16)
            st = [jnp.dot(jnp.concatenate([hi[:, h * bq:(h + 1) * bq], lo[:, h * bq:(h + 1) * bq]], axis=1),
                          mcat, preferred_element_type=jnp.float32) for h in range(2)]
            suffix = jnp.concatenate([st[0][:, :bq], st[1][:, :bq]], axis=1)
            total = jnp.concatenate([st[0][:, bq:], st[1][:, bq:]], axis=1)
            if first:
                w = jnp.where(causal, jnp.exp(lb + suffix), 0.0)
                c_new = total
            else:
                c = c_ref[p]
                w = jnp.exp(lb + suffix + c)
                c_new = c + total
            pv = jnp.dot(w.astype(jnp.bfloat16), split_heads(v_ref[rows, lanes]),
                         preferred_element_type=jnp.float32)
            acc_ref[p] = pv if first else acc_ref[p] + pv
            c_ref[p] = c_new
            cm = jnp.maximum(c_new[:, :bq], c_new[:, bq:])
            cmax = cm if cmax is None else jnp.maximum(cmax, cm)
        return jnp.max(cmax)

    cm0 = tile(i, True)

    def cond(st):
        j, cm = st
        return jnp.logical_and(j >= 0, cm >= F32_EXP_ZERO_BELOW)

    def body(st):
        j, _ = st
        return j - 1, tile(j, False)

    lax.while_loop(cond, body, (i - 1, cm0))
    o_ref[...] = jnp.concatenate([acc_ref[p] for p in range(pairs)], axis=1).astype(o_ref.dtype)


def _attention(qkv, mcat, batch, lp, bq):
    t = qkv.shape[0]
    aw = A_HEADS * A_HEAD_DIM
    pairs = aw // LANES
    nq = lp // bq
    return pl.pallas_call(
        functools.partial(_attn_kernel, bq=bq, pairs=pairs),
        grid=(batch, nq),
        in_specs=[pl.BlockSpec((bq, aw), lambda b, i: (b * nq + i, 0)),
                  pl.BlockSpec((lp, aw), lambda b, i: (b, 1)),
                  pl.BlockSpec((lp, aw), lambda b, i: (b, 2)),
                  pl.BlockSpec((2 * bq, 2 * bq), lambda b, i: (0, 0))],
        out_specs=pl.BlockSpec((bq, aw), lambda b, i: (b * nq + i, 0)),
        out_shape=jax.ShapeDtypeStruct((t, aw), jnp.bfloat16),
        scratch_shapes=[pltpu.VMEM((pairs, bq, 2 * bq), jnp.float32),
                        pltpu.VMEM((pairs, bq, LANES), jnp.float32)],
        compiler_params=_cparams("parallel", "arbitrary"),
        name="sb_attention",
    )(qkv, qkv, qkv, mcat)


def _about_kernel(a_ref, bb_ref, halo_ref, cw_ref, w_ref, h_ref, o_ref, *, nb, bw):
    i = pl.program_id(0)
    bb = bb_ref[...]
    gate_b, gate_c, hb = bb[:, :bw], bb[:, bw:2 * bw], bb[:, 2 * bw:]
    u = gate_c * hb
    hal = halo_ref[...]
    uh = hal[:, bw:2 * bw] * hal[:, 2 * bw:]
    uh = jnp.where(i % nb == 0, 0.0, uh)
    ucat = jnp.concatenate([uh, u], axis=0)
    tm = u.shape[0]
    cw = cw_ref[...]
    conv = u * cw[B_CONV - 1:B_CONV, :]
    for s in range(1, B_CONV):
        us = pltpu.roll(ucat, s, 0)[SUBLANES:SUBLANES + tm, :]
        conv = conv + us * cw[B_CONV - 1 - s:B_CONV - s, :]
    b_out = (gate_b * conv).astype(jnp.bfloat16)
    w = w_ref[...]
    aw = a_ref.shape[1]
    o_ref[...] = (h_ref[...]
                  + jnp.dot(a_ref[...], w[:aw, :], preferred_element_type=jnp.float32)
                  + jnp.dot(b_out, w[aw:, :], preferred_element_type=jnp.float32))


def _about(a_out, bb, conv_w, w_out, h, lp, tm):
    t, d = h.shape
    aw = a_out.shape[1]
    bw = conv_w.shape[1]
    nb = lp // tm
    hb = tm // SUBLANES
    return pl.pallas_call(
        functools.partial(_about_kernel, nb=nb, bw=bw),
        grid=(t // tm,),
        in_specs=[pl.BlockSpec((tm, aw), lambda i: (i, 0)),
                  pl.BlockSpec((tm, 3 * bw), lambda i: (i, 0)),
                  pl.BlockSpec((SUBLANES, 3 * bw), lambda i: (jnp.maximum(i * hb - 1, 0), 0)),
                  pl.BlockSpec((B_CONV, bw), lambda i: (0, 0)),
                  pl.BlockSpec((aw + bw, d), lambda i: (0, 0)),
                  pl.BlockSpec((tm, d), lambda i: (i, 0))],
        out_specs=pl.BlockSpec((tm, d), lambda i: (i, 0)),
        out_shape=jax.ShapeDtypeStruct((t, d), jnp.float32),
        compiler_params=_cparams("parallel"),
        name="ab_out",
    )(a_out, bb, bb, conv_w, w_out, h)


CF_HALO = 32


def _cfout_kernel(z_ref, halo_ref, cw_ref, cb_ref, lg_ref, lb_ref, w_ref, b_ref, h_ref, o_ref,
                  zs_ref, cv_ref, *, nb, rc):
    i = pl.program_id(0)
    tm, c = z_ref.shape
    zs_ref[0:CF_HALO, :] = jnp.where(i % nb == 0, 0.0, halo_ref[...])
    zs_ref[CF_HALO:, :] = z_ref[...]

    def conv_cols(j, _):
        cols = pl.ds(pl.multiple_of(j * LANES, LANES), LANES)
        cw = cw_ref[:, cols]
        cb = cb_ref[:, cols]
        for r in range(0, tm, rc):
            acc = jnp.zeros((rc, LANES), jnp.float32) + cb
            for k in range(CF_CONV):
                off = r + CF_HALO - (CF_CONV - 1) + k
                acc = acc + zs_ref[off:off + rc, cols] * cw[k:k + 1, :]
            cv_ref[r:r + rc, cols] = acc
        return 0

    lax.fori_loop(0, c // LANES, conv_cols, 0)
    acc = cv_ref[...]
    mu = jnp.mean(acc, axis=-1, keepdims=True)
    xc = acc - mu
    var = jnp.mean(xc * xc, axis=-1, keepdims=True)
    y = xc * lax.rsqrt(var + EPS) * lg_ref[...] + lb_ref[...]
    y = y * jax.nn.sigmoid(y)
    o_ref[...] = (h_ref[...] + b_ref[...]
                  + jnp.dot(y.astype(jnp.bfloat16), w_ref[...], preferred_element_type=jnp.float32))


def _cfout(z, conv_w, conv_b, ln_g, ln_b, w2, b2, h, lp, tm):
    t, d = h.shape
    c = z.shape[1]
    nb = lp // tm
    hb = tm // CF_HALO
    vec = lambda n: pl.BlockSpec((1, n), lambda i: (0, 0))
    return pl.pallas_call(
        functools.partial(_cfout_kernel, nb=nb, rc=BLOCK_Q),
        grid=(t // tm,),
        in_specs=[pl.BlockSpec((tm, c), lambda i: (i, 0)),
                  pl.BlockSpec((CF_HALO, c), lambda i: (jnp.maximum(i * hb - 1, 0), 0)),
                  pl.BlockSpec((CF_CONV, c), lambda i: (0, 0)),
                  vec(c), vec(c), vec(c),
                  pl.BlockSpec((c, d), lambda i: (0, 0)),
                  vec(d),
                  pl.BlockSpec((tm, d), lambda i: (i, 0))],
        out_specs=pl.BlockSpec((tm, d), lambda i: (i, 0)),
        out_shape=jax.ShapeDtypeStruct((t, d), jnp.float32),
        scratch_shapes=[pltpu.VMEM((tm + CF_HALO, c), jnp.float32),
                        pltpu.VMEM((tm, c), jnp.float32)],
        compiler_params=_cparams("parallel"),
        name="cf_out",
    )(z, z, conv_w, conv_b, ln_g, ln_b, w2, b2, h)


_NEG = float("-inf")
_BIGKEY = float(2 ** 30)
G_ROW_PAD = 4
G_ROW_STRIDE = N_KEYS + G_ROW_PAD
TOPK_ILP = 4


def _extract_topk(x, key, k):
    vals, keys = [], []
    for _ in range(k):
        m = jnp.max(x, axis=0, keepdims=True)
        km = jnp.min(jnp.where(x == m, key, _BIGKEY), axis=0, keepdims=True)
        x = jnp.where(key == km, _NEG, x)
        vals.append(m)
        keys.append(km)
    return jnp.concatenate(vals, axis=0), jnp.concatenate(keys, axis=0)


def _route_kernel(h_ref, g_ref, wq_ref, sk_ref, gm_ref, s_ref, val_ref, idx_ref, sel_ref, selt_ref,
                  gs_ref):
    tr = h_ref.shape[0]
    nk = N_KEYS
    topk = PEER_TOPK
    nhp = 2 * PEER_HEADS
    y = _rms(h_ref[...], g_ref[...]).astype(jnp.bfloat16)
    q = jnp.dot(y, wq_ref[...], preferred_element_type=jnp.float32).astype(jnp.bfloat16)
    for hp in range(nhp):
        s_ref[hp] = lax.dot_general(sk_ref[hp % 2], q[:, hp * nk:(hp + 1) * nk], _NT,
                                    preferred_element_type=jnp.float32)

    key_iota = lax.broadcasted_iota(jnp.int32, (nk, tr), 0).astype(jnp.float32)

    def level1(n, _):
        for s in range(TOPK_ILP):
            v, ix = _extract_topk(s_ref[TOPK_ILP * n + s], key_iota, topk)
            val_ref[TOPK_ILP * n + s] = v
            idx_ref[TOPK_ILP * n + s] = ix
        return 0

    lax.fori_loop(0, nhp // TOPK_ILP, level1, 0)

    half = topk // 2
    sub8 = lax.broadcasted_iota(jnp.int32, (half, tr), 0).astype(jnp.float32)
    sub16 = lax.broadcasted_iota(jnp.int32, (topk, tr), 0).astype(jnp.float32)

    def level2(n, _):
        for s in range(TOPK_ILP):
            hd = TOPK_ILP * n + s
            v0, v1 = val_ref[2 * hd], val_ref[2 * hd + 1]
            i0, i1 = idx_ref[2 * hd], idx_ref[2 * hd + 1]
            code = lambda pos, a, b: pos * float(nk * nk) + a * float(nk) + b
            cand = [v0[0:1] + v1]
            keys = [code(sub16, i0[0:1], i1)]
            for k0 in range(1, half):
                cand.append(v0[k0:k0 + 1] + v1[:half])
                keys.append(code(k0 * topk + sub8, i0[k0:k0 + 1], i1[:half]))
            cand.append(v0[half:] + v1[0:1])
            keys.append(code((half + sub8) * topk, i0[half:], i1[0:1]))
            tv, tk = _extract_topk(jnp.concatenate(cand, axis=0), jnp.concatenate(keys, axis=0), topk)
            e = jnp.exp(tv - tv[0:1])
            gate = e / jnp.sum(e, axis=0, keepdims=True)
            expert = tk.astype(jnp.int32) & (nk * nk - 1)
            rows = pl.ds(pl.multiple_of(hd * topk, topk), topk)
            sel_ref[0, rows, :] = gate
            sel_ref[1, rows, :] = (expert >> 7).astype(jnp.float32)
            sel_ref[2, rows, :] = (expert & (nk - 1)).astype(jnp.float32)
        return 0

    lax.fori_loop(0, PEER_HEADS // TOPK_ILP, level2, 0)

    for a in range(3):
        selt_ref[a] = sel_ref[a].T

    mrows = nk + 2 * G_ROW_PAD
    sub_l = lax.broadcasted_iota(jnp.int32, (mrows, LANES), 0).astype(jnp.float32)
    sub_r = lax.broadcasted_iota(jnp.int32, (nk, LANES), 0).astype(jnp.float32)
    zblk = jnp.zeros((nk, LANES), jnp.bfloat16)

    def token_pair(tp, _):
        lhs, rhs = [], []
        for s in range(2):
            t = 2 * tp + s
            gt = selt_ref[0, pl.ds(t, 1), :]
            it0 = selt_ref[1, pl.ds(t, 1), :] + float(s * G_ROW_PAD)
            it1 = selt_ref[2, pl.ds(t, 1), :]
            lhs.append(jnp.where(sub_l == it0, 1.0, 0.0).astype(jnp.bfloat16))
            rhs.append(jnp.where(sub_r == it1, gt, 0.0).astype(jnp.bfloat16))
        lhs2 = jnp.concatenate(lhs, axis=1)
        rhs2 = jnp.concatenate([jnp.concatenate([rhs[0], zblk], axis=1),
                                jnp.concatenate([zblk, rhs[1]], axis=1)], axis=0)
        g2 = lax.dot_general(lhs2, rhs2, _NT, preferred_element_type=jnp.float32)
        base = pl.multiple_of(2 * tp * G_ROW_STRIDE, SUBLANES)
        gs_ref[pl.ds(base, nk), :] = g2[:nk, :nk]
        gs_ref[pl.ds(base + nk, mrows), :] = g2[:, nk:]
        return 0

    lax.fori_loop(0, tr // 2, token_pair, 0, unroll=16)

    for i in range(nk):
        gm_ref[:, i * nk:(i + 1) * nk] = gs_ref[pl.ds(i, tr, stride=G_ROW_STRIDE), :].astype(gm_ref.dtype)


def _route(h, g, wq, sk):
    t, d = h.shape
    nq = wq.shape[1]
    nk = N_KEYS
    tr = LANES
    nhp = 2 * PEER_HEADS
    hk = PEER_HEADS * PEER_TOPK
    return pl.pallas_call(
        _route_kernel,
        grid=(t // tr,),
        in_specs=[pl.BlockSpec((tr, d), lambda i: (i, 0)),
                  pl.BlockSpec((1, d), lambda i: (0, 0)),
                  pl.BlockSpec((d, nq), lambda i: (0, 0)),
                  pl.BlockSpec((2, nk, nk), lambda i: (0, 0, 0))],
        out_specs=pl.BlockSpec((tr, nk * nk), lambda i: (i, 0)),
        out_shape=jax.ShapeDtypeStruct((t, nk * nk), jnp.bfloat16),
        scratch_shapes=[pltpu.VMEM((nhp, nk, tr), jnp.float32),
                        pltpu.VMEM((nhp, PEER_TOPK, tr), jnp.float32),
                        pltpu.VMEM((nhp, PEER_TOPK, tr), jnp.float32),
                        pltpu.VMEM((3, hk, tr), jnp.float32),
                        pltpu.VMEM((3, tr, hk), jnp.float32),
                        pltpu.VMEM((tr * G_ROW_STRIDE, nk), jnp.float32)],
        compiler_params=_cparams("parallel"),
        name="peer_route",
    )(h, g, wq, sk)


def _peer_kernel(h_ref, g_ref, gm_ref, ut_ref, v_ref, fg_ref, o_ref, y_ref, acc_ref, *, final_norm, sub):
    e = pl.program_id(1)

    @pl.when(e == 0)
    def _():
        y_ref[...] = _rms(h_ref[...], g_ref[...]).astype(jnp.bfloat16)
        acc_ref[...] = jnp.zeros_like(acc_ref)

    y = y_ref[...]
    contrib = None
    for c in range(0, ut_ref.shape[1], sub):
        hid = jnp.dot(y, ut_ref[:, c:c + sub], preferred_element_type=jnp.float32)
        act = 0.5 * hid * (1.0 + lax.erf(hid * (1.0 / math.sqrt(2.0))))
        a = (gm_ref[:, c:c + sub].astype(jnp.float32) * act).astype(jnp.bfloat16)
        part = jnp.dot(a, v_ref[c:c + sub, :], preferred_element_type=jnp.float32)
        contrib = part if contrib is None else contrib + part
    acc_ref[...] += contrib

    @pl.when(e == pl.num_programs(1) - 1)
    def _():
        out = h_ref[...] + acc_ref[...]
        if final_norm:
            out = _rms(out, fg_ref[...])
        o_ref[...] = out


def _peer(h, g, gm, ut, v, fg, tb, eb, final_norm):
    t, d = h.shape
    ne = v.shape[0]
    return pl.pallas_call(
        functools.partial(_peer_kernel, final_norm=final_norm, sub=2 * LANES),
        grid=(t // tb, ne // eb),
        in_specs=[pl.BlockSpec((tb, d), lambda i, e: (i, 0)),
                  pl.BlockSpec((1, d), lambda i, e: (0, 0)),
                  pl.BlockSpec((tb, eb), lambda i, e: (i, e)),
                  pl.BlockSpec((d, eb), lambda i, e: (0, e)),
                  pl.BlockSpec((eb, d), lambda i, e: (e, 0)),
                  pl.BlockSpec((1, d), lambda i, e: (0, 0))],
        out_specs=pl.BlockSpec((tb, d), lambda i, e: (i, 0)),
        out_shape=jax.ShapeDtypeStruct((t, d), jnp.float32),
        scratch_shapes=[pltpu.VMEM((tb, d), jnp.bfloat16),
                        pltpu.VMEM((tb, d), jnp.float32)],
        compiler_params=_cparams("parallel", "arbitrary"),
        name="peer_experts",
    )(h, g, gm, ut, v, fg)


def _seq_block(lp):
    return 3 * BLOCK_Q if lp % (3 * BLOCK_Q) == 0 else BLOCK_Q


def _forward(x, meta_tokens, mix_norm_g, ffn_norm_g, final_norm_g, ab_w_in, ab_conv_w,
             ab_w_out, cf_w_pw1, cf_b_pw1, cf_conv_w, cf_conv_b, cf_ln_g, cf_ln_b,
             cf_w_pw2, cf_b_pw2, peer_w_q, peer_sub_keys, peer_u, peer_v,
             peer_tb, peer_eb):
    bsz, s, d = x.shape
    l = s + N_META
    lp = -(-l // BLOCK_Q) * BLOCK_Q
    t = bsz * lp
    depth = mix_norm_g.shape[0]
    seq_blk = _seq_block(lp)
    bf = jnp.bfloat16
    row = lambda a: a.reshape(1, -1)

    h = jnp.concatenate([
        jnp.broadcast_to(meta_tokens.astype(x.dtype)[None], (bsz, N_META, d)),
        x,
        jnp.zeros((bsz, lp - l, d), x.dtype)], axis=1).reshape(t, d)

    a_width = A_HEADS * A_HEAD_DIM
    tri = (jnp.arange(BLOCK_Q)[:, None] > jnp.arange(BLOCK_Q)[None, :]).astype(bf)
    mhalf = jnp.concatenate([tri, jnp.ones((BLOCK_Q, BLOCK_Q), bf)], axis=1)
    mcat = jnp.concatenate([mhalf, mhalf], axis=0)
    qscale = jnp.concatenate([jnp.full((a_width,), 1.0 / math.sqrt(A_HEAD_DIM), jnp.float32),
                              jnp.ones((ab_w_in.shape[2] - a_width,), jnp.float32)])

    for i in range(depth):
        j = i // 2
        if i % 2 == 0:
            w_in = (ab_w_in[j] * qscale[None, :]).astype(bf)
            qkv, bb = _inproj(h, row(mix_norm_g[i]), w_in, 3 * a_width, seq_blk)
            a_out = _attention(qkv, mcat, bsz, lp, BLOCK_Q)
            h = _about(a_out, bb, ab_conv_w[j], ab_w_out[j].astype(bf), h, lp, seq_blk)
        else:
            z = _glu(h, row(mix_norm_g[i]), cf_w_pw1[j].astype(bf), row(cf_b_pw1[j]), seq_blk)
            h = _cfout(z, cf_conv_w[j], row(cf_conv_b[j]), row(cf_ln_g[j]), row(cf_ln_b[j]),
                       cf_w_pw2[j].astype(bf), row(cf_b_pw2[j]), h, lp, seq_blk)
        gm = _route(h, row(ffn_norm_g[i]), peer_w_q[i].astype(bf), peer_sub_keys[i].astype(bf))
        h = _peer(h, row(ffn_norm_g[i]), gm,
                  peer_u[i].astype(bf).T, peer_v[i].astype(bf), row(final_norm_g),
                  peer_tb, peer_eb, final_norm=(i == depth - 1))
    return h.reshape(bsz, lp, d)[:, N_META:N_META + s, :]


def kernel(x, meta_tokens, mix_norm_g, ffn_norm_g, final_norm_g, ab_w_in, ab_conv_w, ab_w_out, cf_w_pw1, cf_b_pw1, cf_conv_w, cf_conv_b, cf_ln_g, cf_ln_b, cf_w_pw2, cf_b_pw2, peer_w_q, peer_sub_keys, peer_u, peer_v):
    return _forward(x, meta_tokens, mix_norm_g, ffn_norm_g, final_norm_g, ab_w_in, ab_conv_w,
                    ab_w_out, cf_w_pw1, cf_b_pw1, cf_conv_w, cf_conv_b, cf_ln_g, cf_ln_b,
                    cf_w_pw2, cf_b_pw2, peer_w_q, peer_sub_keys, peer_u, peer_v,
                    peer_tb=768, peer_eb=1024)
```

```python
import functools
import math

import jax
import jax.numpy as jnp
from jax import lax
from jax.experimental import pallas as pl
from jax.experimental.pallas import tpu as pltpu

N_META = 16
BLOCK_Q = 128
A_HEADS = 8
A_HEAD_DIM = 64
B_CONV = 3
CF_CONV = 31
PEER_HEADS = 8
N_KEYS = 128
PEER_TOPK = 16
EPS = 1e-6

LANES = 128
SUBLANES = 8
VMEM_LIMIT = 48 * 1024 * 1024

_NT = (((1,), (1,)), ((), ()))


def _cparams(*sem):
    return pltpu.CompilerParams(dimension_semantics=sem, vmem_limit_bytes=VMEM_LIMIT)


def _rms(x, g):
    return x * lax.rsqrt(jnp.mean(x * x, axis=-1, keepdims=True) + EPS) * g


def _inproj_kernel(h_ref, g_ref, w_ref, qkv_ref, bb_ref, *, a3):
    y = _rms(h_ref[...], g_ref[...]).astype(jnp.bfloat16)
    p = jnp.dot(y, w_ref[...], preferred_element_type=jnp.float32)
    qkv_ref[...] = p[:, :a3].astype(jnp.bfloat16)
    bb_ref[...] = p[:, a3:]


def _inproj(h, g, w, a3, tm):
    t, d = h.shape
    n = w.shape[1]
    return pl.pallas_call(
        functools.partial(_inproj_kernel, a3=a3),
        grid=(t // tm,),
        in_specs=[pl.BlockSpec((tm, d), lambda i: (i, 0)),
                  pl.BlockSpec((1, d), lambda i: (0, 0)),
                  pl.BlockSpec((d, n), lambda i: (0, 0))],
        out_specs=[pl.BlockSpec((tm, a3), lambda i: (i, 0)),
                   pl.BlockSpec((tm, n - a3), lambda i: (i, 0))],
        out_shape=[jax.ShapeDtypeStruct((t, a3), jnp.bfloat16),
                   jax.ShapeDtypeStruct((t, n - a3), jnp.float32)],
        compiler_params=_cparams("parallel"),
        name="ab_inproj",
    )(h, g, w)


def _glu_kernel(h_ref, g_ref, w_ref, b_ref, z_ref, *, inner):
    y = _rms(h_ref[...], g_ref[...]).astype(jnp.bfloat16)
    p = jnp.dot(y, w_ref[...], preferred_element_type=jnp.float32) + b_ref[...]
    z_ref[...] = p[:, :inner] * jax.nn.sigmoid(p[:, inner:])


def _glu(h, g, w, b, tm):
    t, d = h.shape
    n = w.shape[1]
    inner = n // 2
    return pl.pallas_call(
        functools.partial(_glu_kernel, inner=inner),
        grid=(t // tm,),
        in_specs=[pl.BlockSpec((tm, d), lambda i: (i, 0)),
                  pl.BlockSpec((1, d), lambda i: (0, 0)),
                  pl.BlockSpec((d, n), lambda i: (0, 0)),
                  pl.BlockSpec((1, n), lambda i: (0, 0))],
        out_specs=pl.BlockSpec((tm, inner), lambda i: (i, 0)),
        out_shape=jax.ShapeDtypeStruct((t, inner), jnp.float32),
        compiler_params=_cparams("parallel"),
        name="cf_glu",
    )(h, g, w, b)


F32_EXP_ZERO_BELOW = -104.0


def _attn_kernel(q_ref, k_ref, v_ref, m_ref, o_ref, c_ref, acc_ref, *, bq, pairs):
    i = pl.program_id(1)
    head0 = lax.broadcasted_iota(jnp.int32, (bq, LANES), 1) < A_HEAD_DIM
    row = lax.broadcasted_iota(jnp.int32, (bq, 2 * bq), 0)
    col = lax.broadcasted_iota(jnp.int32, (bq, 2 * bq), 1) & (bq - 1)
    causal = col < row
    mcat = m_ref[...]

    def split_heads(x):
        zero = jnp.zeros_like(x)
        return jnp.concatenate([jnp.where(head0, x, zero), jnp.where(head0, zero, x)], axis=0)

    def tile(j, first):
        rows = pl.ds(pl.multiple_of(j * bq, bq), bq)
        cmax = None
        for p in range(pairs):
            lanes = slice(p * LANES, (p + 1) * LANES)
            z = lax.dot_general(q_ref[:, lanes], split_heads(k_ref[rows, lanes]), _NT,
                                preferred_element_type=jnp.float32)
            lb = jnp.minimum(z, 0.0) - jnp.log(1.0 + jnp.exp(-jnp.abs(z)))
            ln = lb - z
            if first:
                ln = jnp.where(causal, ln, 0.0)
            hi = ln.astype(jnp.bfloat16)
            lo = (ln - hi.astype(jnp.float32)).astype(jnp.bfloat16)
            st = [jnp.dot(jnp.concatenate([hi[:, h * bq:(h + 1) * bq], lo[:, h * bq:(h + 1) * bq]], axis=1),
                          mcat, preferred_element_type=jnp.float32) for h in range(2)]
            suffix = jnp.concatenate([st[0][:, :bq], st[1][:, :bq]], axis=1)
            total = jnp.concatenate([st[0][:, bq:], st[1][:, bq:]], axis=1)
            if first:
                w = jnp.where(causal, jnp.exp(lb + suffix), 0.0)
                c_new = total
            else:
                c = c_ref[p]
                w = jnp.exp(lb + suffix + c)
                c_new = c + total
            pv = jnp.dot(w.astype(jnp.bfloat16), split_heads(v_ref[rows, lanes]),
                         preferred_element_type=jnp.float32)
            acc_ref[p] = pv if first else acc_ref[p] + pv
            c_ref[p] = c_new
            cm = jnp.maximum(c_new[:, :bq], c_new[:, bq:])
            cmax = cm if cmax is None else jnp.maximum(cmax, cm)
        return jnp.max(cmax)

    cm0 = tile(i, True)

    def cond(st):
        j, cm = st
        return jnp.logical_and(j >= 0, cm >= F32_EXP_ZERO_BELOW)

    def body(st):
        j, _ = st
        return j - 1, tile(j, False)

    lax.while_loop(cond, body, (i - 1, cm0))
    o_ref[...] = jnp.concatenate([acc_ref[p] for p in range(pairs)], axis=1).astype(o_ref.dtype)


def _attention(qkv, mcat, batch, lp, bq):
    t = qkv.shape[0]
    aw = A_HEADS * A_HEAD_DIM
    pairs = aw // LANES
    nq = lp // bq
    return pl.pallas_call(
        functools.partial(_attn_kernel, bq=bq, pairs=pairs),
        grid=(batch, nq),
        in_specs=[pl.BlockSpec((bq, aw), lambda b, i: (b * nq + i, 0)),
                  pl.BlockSpec((lp, aw), lambda b, i: (b, 1)),
                  pl.BlockSpec((lp, aw), lambda b, i: (b, 2)),
                  pl.BlockSpec((2 * bq, 2 * bq), lambda b, i: (0, 0))],
        out_specs=pl.BlockSpec((bq, aw), lambda b, i: (b * nq + i, 0)),
        out_shape=jax.ShapeDtypeStruct((t, aw), jnp.bfloat16),
        scratch_shapes=[pltpu.VMEM((pairs, bq, 2 * bq), jnp.float32),
                        pltpu.VMEM((pairs, bq, LANES), jnp.float32)],
        compiler_params=_cparams("parallel", "arbitrary"),
        name="sb_attention",
    )(qkv, qkv, qkv, mcat)


def _about_kernel(a_ref, bb_ref, halo_ref, cw_ref, w_ref, h_ref, o_ref, *, nb, bw):
    i = pl.program_id(0)
    bb = bb_ref[...]
    gate_b, gate_c, hb = bb[:, :bw], bb[:, bw:2 * bw], bb[:, 2 * bw:]
    u = gate_c * hb
    hal = halo_ref[...]
    uh = hal[:, bw:2 * bw] * hal[:, 2 * bw:]
    uh = jnp.where(i % nb == 0, 0.0, uh)
    ucat = jnp.concatenate([uh, u], axis=0)
    tm = u.shape[0]
    cw = cw_ref[...]
    conv = u * cw[B_CONV - 1:B_CONV, :]
    for s in range(1, B_CONV):
        us = pltpu.roll(ucat, s, 0)[SUBLANES:SUBLANES + tm, :]
        conv = conv + us * cw[B_CONV - 1 - s:B_CONV - s, :]
    b_out = (gate_b * conv).astype(jnp.bfloat16)
    w = w_ref[...]
    aw = a_ref.shape[1]
    o_ref[...] = (h_ref[...]
                  + jnp.dot(a_ref[...], w[:aw, :], preferred_element_type=jnp.float32)
                  + jnp.dot(b_out, w[aw:, :], preferred_element_type=jnp.float32))


def _about(a_out, bb, conv_w, w_out, h, lp, tm):
    t, d = h.shape
    aw = a_out.shape[1]
    bw = conv_w.shape[1]
    nb = lp // tm
    hb = tm // SUBLANES
    return pl.pallas_call(
        functools.partial(_about_kernel, nb=nb, bw=bw),
        grid=(t // tm,),
        in_specs=[pl.BlockSpec((tm, aw), lambda i: (i, 0)),
                  pl.BlockSpec((tm, 3 * bw), lambda i: (i, 0)),
                  pl.BlockSpec((SUBLANES, 3 * bw), lambda i: (jnp.maximum(i * hb - 1, 0), 0)),
                  pl.BlockSpec((B_CONV, bw), lambda i: (0, 0)),
                  pl.BlockSpec((aw + bw, d), lambda i: (0, 0)),
                  pl.BlockSpec((tm, d), lambda i: (i, 0))],
        out_specs=pl.BlockSpec((tm, d), lambda i: (i, 0)),
        out_shape=jax.ShapeDtypeStruct((t, d), jnp.float32),
        compiler_params=_cparams("parallel"),
        name="ab_out",
    )(a_out, bb, bb, conv_w, w_out, h)


CF_HALO = 32


def _cfout_kernel(z_ref, halo_ref, cw_ref, cb_ref, lg_ref, lb_ref, w_ref, b_ref, h_ref, o_ref,
                  zs_ref, cv_ref, zp_ref, *, nb, rc):
    i = pl.program_id(0)
    tm, c = z_ref.shape
    zs_ref[0:CF_HALO, :] = jnp.where(i % nb == 0, 0.0, halo_ref[...])
    zs_ref[CF_HALO:, :] = z_ref[...]

    span = tm + CF_HALO - SUBLANES

    def conv_cols(j, _):
        cols = pl.ds(pl.multiple_of(j * LANES, LANES), LANES)
        cw = cw_ref[:, cols]
        cb = cb_ref[:, cols]
        for ph in range(1, SUBLANES):
            zp_ref[ph, 0:span, :] = zs_ref[ph:ph + span, cols]
        for r in range(0, tm, rc):
            acc = jnp.zeros((rc, LANES), jnp.float32) + cb
            for k in range(CF_CONV):
                off = r + CF_HALO - (CF_CONV - 1) + k
                ph = off % SUBLANES
                tap = zs_ref[off:off + rc, cols] if ph == 0 else zp_ref[ph, off - ph:off - ph + rc, :]
                acc = acc + tap * cw[k:k + 1, :]
            cv_ref[r:r + rc, cols] = acc
        return 0

    lax.fori_loop(0, c // LANES, conv_cols, 0)
    acc = cv_ref[...]
    mu = jnp.mean(acc, axis=-1, keepdims=True)
    xc = acc - mu
    var = jnp.mean(xc * xc, axis=-1, keepdims=True)
    y = xc * lax.rsqrt(var + EPS) * lg_ref[...] + lb_ref[...]
    y = y * jax.nn.sigmoid(y)
    o_ref[...] = (h_ref[...] + b_ref[...]
                  + jnp.dot(y.astype(jnp.bfloat16), w_ref[...], preferred_element_type=jnp.float32))


def _cfout(z, conv_w, conv_b, ln_g, ln_b, w2, b2, h, lp, tm):
    t, d = h.shape
    c = z.shape[1]
    nb = lp // tm
    hb = tm // CF_HALO
    vec = lambda n: pl.BlockSpec((1, n), lambda i: (0, 0))
    return pl.pallas_call(
        functools.partial(_cfout_kernel, nb=nb, rc=BLOCK_Q),
        grid=(t // tm,),
        in_specs=[pl.BlockSpec((tm, c), lambda i: (i, 0)),
                  pl.BlockSpec((CF_HALO, c), lambda i: (jnp.maximum(i * hb - 1, 0), 0)),
                  pl.BlockSpec((CF_CONV, c), lambda i: (0, 0)),
                  vec(c), vec(c), vec(c),
                  pl.BlockSpec((c, d), lambda i: (0, 0)),
                  vec(d),
                  pl.BlockSpec((tm, d), lambda i: (i, 0))],
        out_specs=pl.BlockSpec((tm, d), lambda i: (i, 0)),
        out_shape=jax.ShapeDtypeStruct((t, d), jnp.float32),
        scratch_shapes=[pltpu.VMEM((tm + CF_HALO, c), jnp.float32),
                        pltpu.VMEM((tm, c), jnp.float32),
                        pltpu.VMEM((SUBLANES, tm + CF_HALO, LANES), jnp.float32)],
        compiler_params=_cparams("parallel"),
        name="cf_out",
    )(z, z, conv_w, conv_b, ln_g, ln_b, w2, b2, h)


_NEG = float("-inf")
_BIGKEY = float(2 ** 30)
G_ROW_PAD = 4
G_ROW_STRIDE = N_KEYS + G_ROW_PAD
TOPK_ILP = 4


def _extract_topk(x, key, k):
    vals, keys = [], []
    for _ in range(k):
        m = jnp.max(x, axis=0, keepdims=True)
        km = jnp.min(jnp.where(x == m, key, _BIGKEY), axis=0, keepdims=True)
        x = jnp.where(key == km, _NEG, x)
        vals.append(m)
        keys.append(km)
    return jnp.concatenate(vals, axis=0), jnp.concatenate(keys, axis=0)


def _route_kernel(h_ref, g_ref, wq_ref, sk_ref, gm_ref, s_ref, val_ref, idx_ref, sel_ref, selt_ref,
                  gs_ref):
    tr = h_ref.shape[0]
    nk = N_KEYS
    topk = PEER_TOPK
    nhp = 2 * PEER_HEADS
    y = _rms(h_ref[...], g_ref[...]).astype(jnp.bfloat16)
    q = jnp.dot(y, wq_ref[...], preferred_element_type=jnp.float32).astype(jnp.bfloat16)
    for hp in range(nhp):
        s_ref[hp] = lax.dot_general(sk_ref[hp % 2], q[:, hp * nk:(hp + 1) * nk], _NT,
                                    preferred_element_type=jnp.float32)

    key_iota = lax.broadcasted_iota(jnp.int32, (nk, tr), 0).astype(jnp.float32)

    def level1(n, _):
        for s in range(TOPK_ILP):
            v, ix = _extract_topk(s_ref[TOPK_ILP * n + s], key_iota, topk)
            val_ref[TOPK_ILP * n + s] = v
            idx_ref[TOPK_ILP * n + s] = ix
        return 0

    lax.fori_loop(0, nhp // TOPK_ILP, level1, 0)

    half = topk // 2
    sub8 = lax.broadcasted_iota(jnp.int32, (half, tr), 0).astype(jnp.float32)
    sub16 = lax.broadcasted_iota(jnp.int32, (topk, tr), 0).astype(jnp.float32)

    def level2(n, _):
        for s in range(TOPK_ILP):
            hd = TOPK_ILP * n + s
            v0, v1 = val_ref[2 * hd], val_ref[2 * hd + 1]
            i0, i1 = idx_ref[2 * hd], idx_ref[2 * hd + 1]
            code = lambda pos, a, b: pos * float(nk * nk) + a * float(nk) + b
            cand = [v0[0:1] + v1]
            keys = [code(sub16, i0[0:1], i1)]
            for k0 in range(1, half):
                cand.append(v0[k0:k0 + 1] + v1[:half])
                keys.append(code(k0 * topk + sub8, i0[k0:k0 + 1], i1[:half]))
            cand.append(v0[half:] + v1[0:1])
            keys.append(code((half + sub8) * topk, i0[half:], i1[0:1]))
            tv, tk = _extract_topk(jnp.concatenate(cand, axis=0), jnp.concatenate(keys, axis=0), topk)
            e = jnp.exp(tv - tv[0:1])
            gate = e / jnp.sum(e, axis=0, keepdims=True)
            expert = tk.astype(jnp.int32) & (nk * nk - 1)
            rows = pl.ds(pl.multiple_of(hd * topk, topk), topk)
            sel_ref[0, rows, :] = gate
            sel_ref[1, rows, :] = (expert >> 7).astype(jnp.float32)
            sel_ref[2, rows, :] = (expert & (nk - 1)).astype(jnp.float32)
        return 0

    lax.fori_loop(0, PEER_HEADS // TOPK_ILP, level2, 0)

    for a in range(3):
        selt_ref[a] = sel_ref[a].T

    mrows = nk + 2 * G_ROW_PAD
    sub_l = lax.broadcasted_iota(jnp.int32, (mrows, LANES), 0).astype(jnp.float32)
    sub_r = lax.broadcasted_iota(jnp.int32, (nk, LANES), 0).astype(jnp.float32)
    zblk = jnp.zeros((nk, LANES), jnp.bfloat16)

    def token_pair(tp, _):
        lhs, rhs = [], []
        for s in range(2):
            t = 2 * tp + s
            gt = selt_ref[0, pl.ds(t, 1), :]
            it0 = selt_ref[1, pl.ds(t, 1), :] + float(s * G_ROW_PAD)
            it1 = selt_ref[2, pl.ds(t, 1), :]
            lhs.append(jnp.where(sub_l == it0, 1.0, 0.0).astype(jnp.bfloat16))
            rhs.append(jnp.where(sub_r == it1, gt, 0.0).astype(jnp.bfloat16))
        lhs2 = jnp.concatenate(lhs, axis=1)
        rhs2 = jnp.concatenate([jnp.concatenate([rhs[0], zblk], axis=1),
                                jnp.concatenate([zblk, rhs[1]], axis=1)], axis=0)
        g2 = lax.dot_general(lhs2, rhs2, _NT, preferred_element_type=jnp.float32)
        base = pl.multiple_of(2 * tp * G_ROW_STRIDE, SUBLANES)
        gs_ref[pl.ds(base, nk), :] = g2[:nk, :nk]
        gs_ref[pl.ds(base + nk, mrows), :] = g2[:, nk:]
        return 0

    lax.fori_loop(0, tr // 2, token_pair, 0, unroll=16)

    for i in range(nk):
        gm_ref[:, i * nk:(i + 1) * nk] = gs_ref[pl.ds(i, tr, stride=G_ROW_STRIDE), :].astype(gm_ref.dtype)


def _route(h, g, wq, sk):
    t, d = h.shape
    nq = wq.shape[1]
    nk = N_KEYS
    tr = LANES
    nhp = 2 * PEER_HEADS
    hk = PEER_HEADS * PEER_TOPK
    return pl.pallas_call(
        _route_kernel,
        grid=(t // tr,),
        in_specs=[pl.BlockSpec((tr, d), lambda i: (i, 0)),
                  pl.BlockSpec((1, d), lambda i: (0, 0)),
                  pl.BlockSpec((d, nq), lambda i: (0, 0)),
                  pl.BlockSpec((2, nk, nk), lambda i: (0, 0, 0))],
        out_specs=pl.BlockSpec((tr, nk * nk), lambda i: (i, 0)),
        out_shape=jax.ShapeDtypeStruct((t, nk * nk), jnp.bfloat16),
        scratch_shapes=[pltpu.VMEM((nhp, nk, tr), jnp.float32),
                        pltpu.VMEM((nhp, PEER_TOPK, tr), jnp.float32),
                        pltpu.VMEM((nhp, PEER_TOPK, tr), jnp.float32),
                        pltpu.VMEM((3, hk, tr), jnp.float32),
                        pltpu.VMEM((3, tr, hk), jnp.float32),
                        pltpu.VMEM((tr * G_ROW_STRIDE, nk), jnp.float32)],
        compiler_params=_cparams("parallel"),
        name="peer_route",
    )(h, g, wq, sk)


def _peer_kernel(h_ref, g_ref, gm_ref, u_ref, v_ref, fg_ref, o_ref, y_ref, acc_ref, *, final_norm):
    e = pl.program_id(1)

    @pl.when(e == 0)
    def _():
        y_ref[...] = _rms(h_ref[...], g_ref[...]).astype(jnp.bfloat16)
        acc_ref[...] = jnp.zeros_like(acc_ref)

    hid = lax.dot_general(y_ref[...], u_ref[...], _NT, preferred_element_type=jnp.float32)
    act = 0.5 * hid * (1.0 + lax.erf(hid * (1.0 / math.sqrt(2.0))))
    a = (gm_ref[...].astype(jnp.float32) * act).astype(jnp.bfloat16)
    acc_ref[...] += jnp.dot(a, v_ref[...], preferred_element_type=jnp.float32)

    @pl.when(e == pl.num_programs(1) - 1)
    def _():
        out = h_ref[...] + acc_ref[...]
        if final_norm:
            out = _rms(out, fg_ref[...])
        o_ref[...] = out


def _peer(h, g, gm, u, v, fg, tb, eb, final_norm):
    t, d = h.shape
    ne = v.shape[0]
    return pl.pallas_call(
        functools.partial(_peer_kernel, final_norm=final_norm),
        grid=(t // tb, ne // eb),
        in_specs=[pl.BlockSpec((tb, d), lambda i, e: (i, 0)),
                  pl.BlockSpec((1, d), lambda i, e: (0, 0)),
                  pl.BlockSpec((tb, eb), lambda i, e: (i, e)),
                  pl.BlockSpec((eb, d), lambda i, e: (e, 0)),
                  pl.BlockSpec((eb, d), lambda i, e: (e, 0)),
                  pl.BlockSpec((1, d), lambda i, e: (0, 0))],
        out_specs=pl.BlockSpec((tb, d), lambda i, e: (i, 0)),
        out_shape=jax.ShapeDtypeStruct((t, d), jnp.float32),
        scratch_shapes=[pltpu.VMEM((tb, d), jnp.bfloat16),
                        pltpu.VMEM((tb, d), jnp.float32)],
        compiler_params=_cparams("parallel", "arbitrary"),
        name="peer_experts",
    )(h, g, gm, u, v, fg)


def _seq_block(lp):
    return 3 * BLOCK_Q if lp % (3 * BLOCK_Q) == 0 else BLOCK_Q


def _forward(x, meta_tokens, mix_norm_g, ffn_norm_g, final_norm_g, ab_w_in, ab_conv_w,
             ab_w_out, cf_w_pw1, cf_b_pw1, cf_conv_w, cf_conv_b, cf_ln_g, cf_ln_b,
             cf_w_pw2, cf_b_pw2, peer_w_q, peer_sub_keys, peer_u, peer_v,
             peer_tb, peer_eb):
    bsz, s, d = x.shape
    l = s + N_META
    lp = -(-l // BLOCK_Q) * BLOCK_Q
    t = bsz * lp
    depth = mix_norm_g.shape[0]
    seq_blk = _seq_block(lp)
    bf = jnp.bfloat16
    row = lambda a: a.reshape(1, -1)

    h = jnp.concatenate([
        jnp.broadcast_to(meta_tokens.astype(x.dtype)[None], (bsz, N_META, d)),
        x,
        jnp.zeros((bsz, lp - l, d), x.dtype)], axis=1).reshape(t, d)

    a_width = A_HEADS * A_HEAD_DIM
    tri = (jnp.arange(BLOCK_Q)[:, None] > jnp.arange(BLOCK_Q)[None, :]).astype(bf)
    mhalf = jnp.concatenate([tri, jnp.ones((BLOCK_Q, BLOCK_Q), bf)], axis=1)
    mcat = jnp.concatenate([mhalf, mhalf], axis=0)
    qscale = jnp.concatenate([jnp.full((a_width,), 1.0 / math.sqrt(A_HEAD_DIM), jnp.float32),
                              jnp.ones((ab_w_in.shape[2] - a_width,), jnp.float32)])

    for i in range(depth):
        j = i // 2
        if i % 2 == 0:
            w_in = (ab_w_in[j] * qscale[None, :]).astype(bf)
            qkv, bb = _inproj(h, row(mix_norm_g[i]), w_in, 3 * a_width, seq_blk)
            a_out = _attention(qkv, mcat, bsz, lp, BLOCK_Q)
            h = _about(a_out, bb, ab_conv_w[j], ab_w_out[j].astype(bf), h, lp, seq_blk)
        else:
            z = _glu(h, row(mix_norm_g[i]), cf_w_pw1[j].astype(bf), row(cf_b_pw1[j]), seq_blk)
            h = _cfout(z, cf_conv_w[j], row(cf_conv_b[j]), row(cf_ln_g[j]), row(cf_ln_b[j]),
                       cf_w_pw2[j].astype(bf), row(cf_b_pw2[j]), h, lp, seq_blk)
        gm = _route(h, row(ffn_norm_g[i]), peer_w_q[i].astype(bf), peer_sub_keys[i].astype(bf))
        h = _peer(h, row(ffn_norm_g[i]), gm,
                  peer_u[i].astype(bf), peer_v[i].astype(bf), row(final_norm_g),
                  peer_tb, peer_eb, final_norm=(i == depth - 1))
    return h.reshape(bsz, lp, d)[:, N_META:N_META + s, :]


def kernel(x, meta_tokens, mix_norm_g, ffn_norm_g, final_norm_g, ab_w_in, ab_conv_w, ab_w_out, cf_w_pw1, cf_b_pw1, cf_conv_w, cf_conv_b, cf_ln_g, cf_ln_b, cf_w_pw2, cf_b_pw2, peer_w_q, peer_sub_keys, peer_u, peer_v):
    return _forward(x, meta_tokens, mix_norm_g, ffn_norm_g, final_norm_g, ab_w_in, ab_conv_w,
                    ab_w_out, cf_w_pw1, cf_b_pw1, cf_conv_w, cf_conv_b, cf_ln_g, cf_ln_b,
                    cf_w_pw2, cf_b_pw2, peer_w_q, peer_sub_keys, peer_u, peer_v,
                    peer_tb=768, peer_eb=1024)
```

```python
import functools
import math

import jax
import jax.numpy as jnp
from jax import lax
from jax.experimental import pallas as pl
from jax.experimental.pallas import tpu as pltpu

N_META = 16
BLOCK_Q = 128
A_HEADS = 8
A_HEAD_DIM = 64
B_CONV = 3
CF_CONV = 31
PEER_HEADS = 8
N_KEYS = 128
PEER_TOPK = 16
EPS = 1e-6

LANES = 128
SUBLANES = 8
VMEM_LIMIT = 48 * 1024 * 1024

_NT = (((1,), (1,)), ((), ()))


def _cparams(*sem):
    return pltpu.CompilerParams(dimension_semantics=sem, vmem_limit_bytes=VMEM_LIMIT)


def _rms(x, g):
    return x * lax.rsqrt(jnp.mean(x * x, axis=-1, keepdims=True) + EPS) * g


def _inproj_kernel(h_ref, g_ref, w_ref, qkv_ref, bb_ref, *, a3):
    y = _rms(h_ref[...], g_ref[...]).astype(jnp.bfloat16)
    p = jnp.dot(y, w_ref[...], preferred_element_type=jnp.float32)
    qkv_ref[...] = p[:, :a3].astype(jnp.bfloat16)
    bb_ref[...] = p[:, a3:]


def _inproj(h, g, w, a3, tm):
    t, d = h.shape
    n = w.shape[1]
    return pl.pallas_call(
        functools.partial(_inproj_kernel, a3=a3),
        grid=(t // tm,),
        in_specs=[pl.BlockSpec((tm, d), lambda i: (i, 0)),
                  pl.BlockSpec((1, d), lambda i: (0, 0)),
                  pl.BlockSpec((d, n), lambda i: (0, 0))],
        out_specs=[pl.BlockSpec((tm, a3), lambda i: (i, 0)),
                   pl.BlockSpec((tm, n - a3), lambda i: (i, 0))],
        out_shape=[jax.ShapeDtypeStruct((t, a3), jnp.bfloat16),
                   jax.ShapeDtypeStruct((t, n - a3), jnp.float32)],
        compiler_params=_cparams("parallel"),
        name="ab_inproj",
    )(h, g, w)


def _glu_kernel(h_ref, g_ref, w_ref, b_ref, z_ref, *, inner):
    y = _rms(h_ref[...], g_ref[...]).astype(jnp.bfloat16)
    p = jnp.dot(y, w_ref[...], preferred_element_type=jnp.float32) + b_ref[...]
    z_ref[...] = p[:, :inner] * jax.nn.sigmoid(p[:, inner:])


def _glu(h, g, w, b, tm):
    t, d = h.shape
    n = w.shape[1]
    inner = n // 2
    return pl.pallas_call(
        functools.partial(_glu_kernel, inner=inner),
        grid=(t // tm,),
        in_specs=[pl.BlockSpec((tm, d), lambda i: (i, 0)),
                  pl.BlockSpec((1, d), lambda i: (0, 0)),
                  pl.BlockSpec((d, n), lambda i: (0, 0)),
                  pl.BlockSpec((1, n), lambda i: (0, 0))],
        out_specs=pl.BlockSpec((tm, inner), lambda i: (i, 0)),
        out_shape=jax.ShapeDtypeStruct((t, inner), jnp.float32),
        compiler_params=_cparams("parallel"),
        name="cf_glu",
    )(h, g, w, b)


F32_EXP_ZERO_BELOW = -104.0


def _attn_kernel(q_ref, k_ref, v_ref, m_ref, o_ref, c_ref, acc_ref, *, bq, pairs):
    i = pl.program_id(1)
    head0 = lax.broadcasted_iota(jnp.int32, (bq, LANES), 1) < A_HEAD_DIM
    row = lax.broadcasted_iota(jnp.int32, (bq, 2 * bq), 0)
    col = lax.broadcasted_iota(jnp.int32, (bq, 2 * bq), 1) & (bq - 1)
    causal = col < row
    mcat = m_ref[...]

    def split_heads(x):
        zero = jnp.zeros_like(x)
        return jnp.concatenate([jnp.where(head0, x, zero), jnp.where(head0, zero, x)], axis=0)

    def tile(j, first):
        rows = pl.ds(pl.multiple_of(j * bq, bq), bq)
        cmax = None
        for p in range(pairs):
            lanes = slice(p * LANES, (p + 1) * LANES)
            z = lax.dot_general(q_ref[:, lanes], split_heads(k_ref[rows, lanes]), _NT,
                                preferred_element_type=jnp.float32)
            lb = jnp.minimum(z, 0.0) - jnp.log(1.0 + jnp.exp(-jnp.abs(z)))
            ln = lb - z
            if first:
                ln = jnp.where(causal, ln, 0.0)
            hi = ln.astype(jnp.bfloat16)
            lo = (ln - hi.astype(jnp.float32)).astype(jnp.bfloat16)
            st = [jnp.dot(jnp.concatenate([hi[:, h * bq:(h + 1) * bq], lo[:, h * bq:(h + 1) * bq]], axis=1),
                          mcat, preferred_element_type=jnp.float32) for h in range(2)]
            suffix = jnp.concatenate([st[0][:, :bq], st[1][:, :bq]], axis=1)
            total = jnp.concatenate([st[0][:, bq:], st[1][:, bq:]], axis=1)
            if first:
                w = jnp.where(causal, jnp.exp(lb + suffix), 0.0)
                c_new = total
            else:
                c = c_ref[p]
                w = jnp.exp(lb + suffix + c)
                c_new = c + total
            pv = jnp.dot(w.astype(jnp.bfloat16), split_heads(v_ref[rows, lanes]),
                         preferred_element_type=jnp.float32)
            acc_ref[p] = pv if first else acc_ref[p] + pv
            c_ref[p] = c_new
            cm = jnp.maximum(c_new[:, :bq], c_new[:, bq:])
            cmax = cm if cmax is None else jnp.maximum(cmax, cm)
        return jnp.max(cmax)

    cm0 = tile(i, True)

    def cond(st):
        j, cm = st
        return jnp.logical_and(j >= 0, cm >= F32_EXP_ZERO_BELOW)

    def body(st):
        j, _ = st
        return j - 1, tile(j, False)

    lax.while_loop(cond, body, (i - 1, cm0))
    o_ref[...] = jnp.concatenate([acc_ref[p] for p in range(pairs)], axis=1).astype(o_ref.dtype)


def _attention(qkv, mcat, batch, lp, bq):
    t = qkv.shape[0]
    aw = A_HEADS * A_HEAD_DIM
    pairs = aw // LANES
    nq = lp // bq
    return pl.pallas_call(
        functools.partial(_attn_kernel, bq=bq, pairs=pairs),
        grid=(batch, nq),
        in_specs=[pl.BlockSpec((bq, aw), lambda b, i: (b * nq + i, 0)),
                  pl.BlockSpec((lp, aw), lambda b, i: (b, 1)),
                  pl.BlockSpec((lp, aw), lambda b, i: (b, 2)),
                  pl.BlockSpec((2 * bq, 2 * bq), lambda b, i: (0, 0))],
        out_specs=pl.BlockSpec((bq, aw), lambda b, i: (b * nq + i, 0)),
        out_shape=jax.ShapeDtypeStruct((t, aw), jnp.bfloat16),
        scratch_shapes=[pltpu.VMEM((pairs, bq, 2 * bq), jnp.float32),
                        pltpu.VMEM((pairs, bq, LANES), jnp.float32)],
        compiler_params=_cparams("parallel", "arbitrary"),
        name="sb_attention",
    )(qkv, qkv, qkv, mcat)


def _about_kernel(a_ref, bb_ref, halo_ref, cw_ref, w_ref, h_ref, o_ref, *, nb, bw):
    i = pl.program_id(0)
    bb = bb_ref[...]
    gate_b, gate_c, hb = bb[:, :bw], bb[:, bw:2 * bw], bb[:, 2 * bw:]
    u = gate_c * hb
    hal = halo_ref[...]
    uh = hal[:, bw:2 * bw] * hal[:, 2 * bw:]
    uh = jnp.where(i % nb == 0, 0.0, uh)
    ucat = jnp.concatenate([uh, u], axis=0)
    tm = u.shape[0]
    cw = cw_ref[...]
    conv = u * cw[B_CONV - 1:B_CONV, :]
    for s in range(1, B_CONV):
        us = pltpu.roll(ucat, s, 0)[SUBLANES:SUBLANES + tm, :]
        conv = conv + us * cw[B_CONV - 1 - s:B_CONV - s, :]
    b_out = (gate_b * conv).astype(jnp.bfloat16)
    w = w_ref[...]
    aw = a_ref.shape[1]
    o_ref[...] = (h_ref[...]
                  + jnp.dot(a_ref[...], w[:aw, :], preferred_element_type=jnp.float32)
                  + jnp.dot(b_out, w[aw:, :], preferred_element_type=jnp.float32))


def _about(a_out, bb, conv_w, w_out, h, lp, tm):
    t, d = h.shape
    aw = a_out.shape[1]
    bw = conv_w.shape[1]
    nb = lp // tm
    hb = tm // SUBLANES
    return pl.pallas_call(
        functools.partial(_about_kernel, nb=nb, bw=bw),
        grid=(t // tm,),
        in_specs=[pl.BlockSpec((tm, aw), lambda i: (i, 0)),
                  pl.BlockSpec((tm, 3 * bw), lambda i: (i, 0)),
                  pl.BlockSpec((SUBLANES, 3 * bw), lambda i: (jnp.maximum(i * hb - 1, 0), 0)),
                  pl.BlockSpec((B_CONV, bw), lambda i: (0, 0)),
                  pl.BlockSpec((aw + bw, d), lambda i: (0, 0)),
                  pl.BlockSpec((tm, d), lambda i: (i, 0))],
        out_specs=pl.BlockSpec((tm, d), lambda i: (i, 0)),
        out_shape=jax.ShapeDtypeStruct((t, d), jnp.float32),
        compiler_params=_cparams("parallel"),
        name="ab_out",
    )(a_out, bb, bb, conv_w, w_out, h)


CF_HALO = 32


def _cfout_kernel(z_ref, halo_ref, cw_ref, cb_ref, lg_ref, lb_ref, w_ref, b_ref, h_ref, o_ref,
                  zs_ref, cv_ref, zp_ref, *, nb, rc):
    i = pl.program_id(0)
    tm, c = z_ref.shape
    zs_ref[0:CF_HALO, :] = jnp.where(i % nb == 0, 0.0, halo_ref[...])
    zs_ref[CF_HALO:, :] = z_ref[...]

    span = tm + CF_HALO - SUBLANES

    def conv_cols(j, _):
        cols = pl.ds(pl.multiple_of(j * LANES, LANES), LANES)
        cw = cw_ref[:, cols]
        cb = cb_ref[:, cols]
        for ph in range(1, SUBLANES):
            zp_ref[ph, 0:span, :] = zs_ref[ph:ph + span, cols]
        for r in range(0, tm, rc):
            acc = jnp.zeros((rc, LANES), jnp.float32) + cb
            for k in range(CF_CONV):
                off = r + CF_HALO - (CF_CONV - 1) + k
                ph = off % SUBLANES
                tap = zs_ref[off:off + rc, cols] if ph == 0 else zp_ref[ph, off - ph:off - ph + rc, :]
                acc = acc + tap * cw[k:k + 1, :]
            cv_ref[r:r + rc, cols] = acc
        return 0

    lax.fori_loop(0, c // LANES, conv_cols, 0)
    acc = cv_ref[...]
    mu = jnp.mean(acc, axis=-1, keepdims=True)
    xc = acc - mu
    var = jnp.mean(xc * xc, axis=-1, keepdims=True)
    y = xc * lax.rsqrt(var + EPS) * lg_ref[...] + lb_ref[...]
    y = y * jax.nn.sigmoid(y)
    o_ref[...] = (h_ref[...] + b_ref[...]
                  + jnp.dot(y.astype(jnp.bfloat16), w_ref[...], preferred_element_type=jnp.float32))


def _cfout(z, conv_w, conv_b, ln_g, ln_b, w2, b2, h, lp, tm):
    t, d = h.shape
    c = z.shape[1]
    nb = lp // tm
    hb = tm // CF_HALO
    vec = lambda n: pl.BlockSpec((1, n), lambda i: (0, 0))
    return pl.pallas_call(
        functools.partial(_cfout_kernel, nb=nb, rc=BLOCK_Q),
        grid=(t // tm,),
        in_specs=[pl.BlockSpec((tm, c), lambda i: (i, 0)),
                  pl.BlockSpec((CF_HALO, c), lambda i: (jnp.maximum(i * hb - 1, 0), 0)),
                  pl.BlockSpec((CF_CONV, c), lambda i: (0, 0)),
                  vec(c), vec(c), vec(c),
                  pl.BlockSpec((c, d), lambda i: (0, 0)),
                  vec(d),
                  pl.BlockSpec((tm, d), lambda i: (i, 0))],
        out_specs=pl.BlockSpec((tm, d), lambda i: (i, 0)),
        out_shape=jax.ShapeDtypeStruct((t, d), jnp.float32),
        scratch_shapes=[pltpu.VMEM((tm + CF_HALO, c), jnp.float32),
                        pltpu.VMEM((tm, c), jnp.float32),
                        pltpu.VMEM((SUBLANES, tm + CF_HALO, LANES), jnp.float32)],
        compiler_params=_cparams("parallel"),
        name="cf_out",
    )(z, z, conv_w, conv_b, ln_g, ln_b, w2, b2, h)


_NEG = float("-inf")
_BIGKEY = float(2 ** 30)
G_ROW_PAD = 4
G_ROW_STRIDE = N_KEYS + G_ROW_PAD
TOPK_ILP = 4


def _extract_topk(x, key, k):
    vals, keys = [], []
    for _ in range(k):
        m = jnp.max(x, axis=0, keepdims=True)
        km = jnp.min(jnp.where(x == m, key, _BIGKEY), axis=0, keepdims=True)
        x = jnp.where(key == km, _NEG, x)
        vals.append(m)
        keys.append(km)
    return jnp.concatenate(vals, axis=0), jnp.concatenate(keys, axis=0)


def _batcher_network(n):
    def merge(lo, hi, r):
        step = 2 * r
        if step < hi - lo:
            yield from merge(lo, hi, step)
            yield from merge(lo + r, hi, step)
            yield from ((i, i + r) for i in range(lo + r, hi - r, step))
        else:
            yield (lo, lo + r)

    def sort(lo, hi):
        if hi - lo >= 1:
            mid = lo + (hi - lo) // 2
            yield from sort(lo, mid)
            yield from sort(mid + 1, hi)
            yield from merge(lo, hi, 1)

    return tuple(sort(0, n - 1))


def _sublane_all(op, x):
    shift = SUBLANES // 2
    while shift:
        x = op(x, pltpu.roll(x, shift, 0))
        shift //= 2
    return x


def _topk_distinct(x, key, k):
    depth = x.shape[0] // SUBLANES
    vs = [x[d * SUBLANES:(d + 1) * SUBLANES] for d in range(depth)]
    ks = [key[d * SUBLANES:(d + 1) * SUBLANES] for d in range(depth)]
    for i, j in _batcher_network(depth):
        swap = vs[j] > vs[i]
        vs[i], vs[j] = jnp.maximum(vs[i], vs[j]), jnp.minimum(vs[i], vs[j])
        ks[i], ks[j] = jnp.where(swap, ks[j], ks[i]), jnp.where(swap, ks[i], ks[j])
    vals, keys = [], []
    tied = jnp.zeros(vs[0].shape, jnp.float32)
    prev = None
    for r in range(k + 1):
        m = _sublane_all(jnp.maximum, vs[0])
        if prev is not None:
            tied = jnp.maximum(tied, jnp.where(m >= prev, 1.0, 0.0))
        prev = m
        if r == k:
            break
        km = _sublane_all(jnp.minimum, jnp.where(vs[0] == m, ks[0], _BIGKEY))
        win = ks[0] == km
        vals.append(m[0:1])
        keys.append(km[0:1])
        for d in range(min(depth, k - r)):
            if d + 1 < depth:
                vs[d] = jnp.where(win, vs[d + 1], vs[d])
                ks[d] = jnp.where(win, ks[d + 1], ks[d])
            else:
                vs[d] = jnp.where(win, _NEG, vs[d])
    return jnp.concatenate(vals, axis=0), jnp.concatenate(keys, axis=0), tied


def _route_kernel(h_ref, g_ref, wq_ref, sk_ref, gm_ref, s_ref, val_ref, idx_ref, sel_ref, selt_ref,
                  gs_ref):
    tr = h_ref.shape[0]
    nk = N_KEYS
    topk = PEER_TOPK
    nhp = 2 * PEER_HEADS
    y = _rms(h_ref[...], g_ref[...]).astype(jnp.bfloat16)
    q = jnp.dot(y, wq_ref[...], preferred_element_type=jnp.float32).astype(jnp.bfloat16)
    for hp in range(nhp):
        s_ref[hp] = lax.dot_general(sk_ref[hp % 2], q[:, hp * nk:(hp + 1) * nk], _NT,
                                    preferred_element_type=jnp.float32)

    key_iota = lax.broadcasted_iota(jnp.int32, (nk, tr), 0).astype(jnp.float32)

    def level1(n, tied):
        for s in range(TOPK_ILP):
            v, ix, td = _topk_distinct(s_ref[TOPK_ILP * n + s], key_iota, topk)
            val_ref[TOPK_ILP * n + s] = v
            idx_ref[TOPK_ILP * n + s] = ix
            tied = jnp.maximum(tied, td)
        return tied

    tied = lax.fori_loop(0, nhp // TOPK_ILP, level1, jnp.zeros((SUBLANES, tr), jnp.float32))

    @pl.when(jnp.max(tied) > 0.0)
    def _():
        def level1_ties(n, _):
            v, ix = _extract_topk(s_ref[n], key_iota, topk)
            val_ref[n] = v
            idx_ref[n] = ix
            return 0

        lax.fori_loop(0, nhp, level1_ties, 0)

    half = topk // 2
    sub8 = lax.broadcasted_iota(jnp.int32, (half, tr), 0).astype(jnp.float32)
    sub16 = lax.broadcasted_iota(jnp.int32, (topk, tr), 0).astype(jnp.float32)

    def level2(n, _):
        for s in range(TOPK_ILP):
            hd = TOPK_ILP * n + s
            v0, v1 = val_ref[2 * hd], val_ref[2 * hd + 1]
            i0, i1 = idx_ref[2 * hd], idx_ref[2 * hd + 1]
            code = lambda pos, a, b: pos * float(nk * nk) + a * float(nk) + b
            cand = [v0[0:1] + v1]
            keys = [code(sub16, i0[0:1], i1)]
            for k0 in range(1, half):
                cand.append(v0[k0:k0 + 1] + v1[:half])
                keys.append(code(k0 * topk + sub8, i0[k0:k0 + 1], i1[:half]))
            cand.append(v0[half:] + v1[0:1])
            keys.append(code((half + sub8) * topk, i0[half:], i1[0:1]))
            tv, tk = _extract_topk(jnp.concatenate(cand, axis=0), jnp.concatenate(keys, axis=0), topk)
            e = jnp.exp(tv - tv[0:1])
            gate = e / jnp.sum(e, axis=0, keepdims=True)
            expert = tk.astype(jnp.int32) & (nk * nk - 1)
            rows = pl.ds(pl.multiple_of(hd * topk, topk), topk)
            sel_ref[0, rows, :] = gate
            sel_ref[1, rows, :] = (expert >> 7).astype(jnp.float32)
            sel_ref[2, rows, :] = (expert & (nk - 1)).astype(jnp.float32)
        return 0

    lax.fori_loop(0, PEER_HEADS // TOPK_ILP, level2, 0)

    for a in range(3):
        selt_ref[a] = sel_ref[a].T

    mrows = nk + 2 * G_ROW_PAD
    sub_l = lax.broadcasted_iota(jnp.int32, (mrows, LANES), 0).astype(jnp.float32)
    sub_r = lax.broadcasted_iota(jnp.int32, (nk, LANES), 0).astype(jnp.float32)
    zblk = jnp.zeros((nk, LANES), jnp.bfloat16)

    def token_pair(tp, _):
        lhs, rhs = [], []
        for s in range(2):
            t = 2 * tp + s
            gt = selt_ref[0, pl.ds(t, 1), :]
            it0 = selt_ref[1, pl.ds(t, 1), :] + float(s * G_ROW_PAD)
            it1 = selt_ref[2, pl.ds(t, 1), :]
            lhs.append(jnp.where(sub_l == it0, 1.0, 0.0).astype(jnp.bfloat16))
            rhs.append(jnp.where(sub_r == it1, gt, 0.0).astype(jnp.bfloat16))
        lhs2 = jnp.concatenate(lhs, axis=1)
        rhs2 = jnp.concatenate([jnp.concatenate([rhs[0], zblk], axis=1),
                                jnp.concatenate([zblk, rhs[1]], axis=1)], axis=0)
        g2 = lax.dot_general(lhs2, rhs2, _NT, preferred_element_type=jnp.float32)
        base = pl.multiple_of(2 * tp * G_ROW_STRIDE, SUBLANES)
        gs_ref[pl.ds(base, nk), :] = g2[:nk, :nk]
        gs_ref[pl.ds(base + nk, mrows), :] = g2[:, nk:]
        return 0

    lax.fori_loop(0, tr // 2, token_pair, 0, unroll=16)

    for i in range(nk):
        gm_ref[:, i * nk:(i + 1) * nk] = gs_ref[pl.ds(i, tr, stride=G_ROW_STRIDE), :].astype(gm_ref.dtype)


def _route(h, g, wq, sk):
    t, d = h.shape
    nq = wq.shape[1]
    nk = N_KEYS
    tr = LANES
    nhp = 2 * PEER_HEADS
    hk = PEER_HEADS * PEER_TOPK
    return pl.pallas_call(
        _route_kernel,
        grid=(t // tr,),
        in_specs=[pl.BlockSpec((tr, d), lambda i: (i, 0)),
                  pl.BlockSpec((1, d), lambda i: (0, 0)),
                  pl.BlockSpec((d, nq), lambda i: (0, 0)),
                  pl.BlockSpec((2, nk, nk), lambda i: (0, 0, 0))],
        out_specs=pl.BlockSpec((tr, nk * nk), lambda i: (i, 0)),
        out_shape=jax.ShapeDtypeStruct((t, nk * nk), jnp.bfloat16),
        scratch_shapes=[pltpu.VMEM((nhp, nk, tr), jnp.float32),
                        pltpu.VMEM((nhp, PEER_TOPK, tr), jnp.float32),
                        pltpu.VMEM((nhp, PEER_TOPK, tr), jnp.float32),
                        pltpu.VMEM((3, hk, tr), jnp.float32),
                        pltpu.VMEM((3, tr, hk), jnp.float32),
                        pltpu.VMEM((tr * G_ROW_STRIDE, nk), jnp.float32)],
        compiler_params=_cparams("parallel"),
        name="peer_route",
    )(h, g, wq, sk)


def _peer_kernel(h_ref, g_ref, gm_ref, u_ref, v_ref, fg_ref, o_ref, y_ref, acc_ref, *, final_norm):
    e = pl.program_id(1)

    @pl.when(e == 0)
    def _():
        y_ref[...] = _rms(h_ref[...], g_ref[...]).astype(jnp.bfloat16)
        acc_ref[...] = jnp.zeros_like(acc_ref)

    hid = lax.dot_general(y_ref[...], u_ref[...], _NT, preferred_element_type=jnp.float32)
    act = 0.5 * hid * (1.0 + lax.erf(hid * (1.0 / math.sqrt(2.0))))
    a = (gm_ref[...].astype(jnp.float32) * act).astype(jnp.bfloat16)
    acc_ref[...] += jnp.dot(a, v_ref[...], preferred_element_type=jnp.float32)

    @pl.when(e == pl.num_programs(1) - 1)
    def _():
        out = h_ref[...] + acc_ref[...]
        if final_norm:
            out = _rms(out, fg_ref[...])
        o_ref[...] = out


def _peer(h, g, gm, u, v, layer, fg, tb, eb, final_norm):
    t, d = h.shape
    ne = v.shape[1]
    return pl.pallas_call(
        functools.partial(_peer_kernel, final_norm=final_norm),
        grid=(t // tb, ne // eb),
        in_specs=[pl.BlockSpec((tb, d), lambda i, e: (i, 0)),
                  pl.BlockSpec((1, d), lambda i, e: (0, 0)),
                  pl.BlockSpec((tb, eb), lambda i, e: (i, e)),
                  pl.BlockSpec((None, eb, d), lambda i, e: (layer, e, 0)),
                  pl.BlockSpec((None, eb, d), lambda i, e: (layer, e, 0)),
                  pl.BlockSpec((1, d), lambda i, e: (0, 0))],
        out_specs=pl.BlockSpec((tb, d), lambda i, e: (i, 0)),
        out_shape=jax.ShapeDtypeStruct((t, d), jnp.float32),
        scratch_shapes=[pltpu.VMEM((tb, d), jnp.bfloat16),
                        pltpu.VMEM((tb, d), jnp.float32)],
        compiler_params=_cparams("parallel", "arbitrary"),
        name="peer_experts",
    )(h, g, gm, u, v, fg)


def _seq_block(lp):
    return 3 * BLOCK_Q if lp % (3 * BLOCK_Q) == 0 else BLOCK_Q


def _forward(x, meta_tokens, mix_norm_g, ffn_norm_g, final_norm_g, ab_w_in, ab_conv_w,
             ab_w_out, cf_w_pw1, cf_b_pw1, cf_conv_w, cf_conv_b, cf_ln_g, cf_ln_b,
             cf_w_pw2, cf_b_pw2, peer_w_q, peer_sub_keys, peer_u, peer_v,
             peer_tb, peer_eb):
    bsz, s, d = x.shape
    l = s + N_META
    lp = -(-l // BLOCK_Q) * BLOCK_Q
    t = bsz * lp
    depth = mix_norm_g.shape[0]
    seq_blk = _seq_block(lp)
    bf = jnp.bfloat16
    row = lambda a: a.reshape(1, -1)

    h = jnp.concatenate([
        jnp.broadcast_to(meta_tokens.astype(x.dtype)[None], (bsz, N_META, d)),
        x,
        jnp.zeros((bsz, lp - l, d), x.dtype)], axis=1).reshape(t, d)

    a_width = A_HEADS * A_HEAD_DIM
    tri = (jnp.arange(BLOCK_Q)[:, None] > jnp.arange(BLOCK_Q)[None, :]).astype(bf)
    mhalf = jnp.concatenate([tri, jnp.ones((BLOCK_Q, BLOCK_Q), bf)], axis=1)
    mcat = jnp.concatenate([mhalf, mhalf], axis=0)
    qscale = jnp.concatenate([jnp.full((a_width,), 1.0 / math.sqrt(A_HEAD_DIM), jnp.float32),
                              jnp.ones((ab_w_in.shape[2] - a_width,), jnp.float32)])

    u_bf, v_bf = peer_u.astype(bf), peer_v.astype(bf)
    for i in range(depth):
        j = i // 2
        if i % 2 == 0:
            w_in = (ab_w_in[j] * qscale[None, :]).astype(bf)
            qkv, bb = _inproj(h, row(mix_norm_g[i]), w_in, 3 * a_width, seq_blk)
            a_out = _attention(qkv, mcat, bsz, lp, BLOCK_Q)
            h = _about(a_out, bb, ab_conv_w[j], ab_w_out[j].astype(bf), h, lp, seq_blk)
        else:
            z = _glu(h, row(mix_norm_g[i]), cf_w_pw1[j].astype(bf), row(cf_b_pw1[j]), seq_blk)
            h = _cfout(z, cf_conv_w[j], row(cf_conv_b[j]), row(cf_ln_g[j]), row(cf_ln_b[j]),
                       cf_w_pw2[j].astype(bf), row(cf_b_pw2[j]), h, lp, seq_blk)
        gm = _route(h, row(ffn_norm_g[i]), peer_w_q[i].astype(bf), peer_sub_keys[i].astype(bf))
        h = _peer(h, row(ffn_norm_g[i]), gm, u_bf, v_bf, i, row(final_norm_g),
                  peer_tb, peer_eb, final_norm=(i == depth - 1))
    return h.reshape(bsz, lp, d)[:, N_META:N_META + s, :]


def kernel(x, meta_tokens, mix_norm_g, ffn_norm_g, final_norm_g, ab_w_in, ab_conv_w, ab_w_out, cf_w_pw1, cf_b_pw1, cf_conv_w, cf_conv_b, cf_ln_g, cf_ln_b, cf_w_pw2, cf_b_pw2, peer_w_q, peer_sub_keys, peer_u, peer_v):
    return _forward(x, meta_tokens, mix_norm_g, ffn_norm_g, final_norm_g, ab_w_in, ab_conv_w,
                    ab_w_out, cf_w_pw1, cf_b_pw1, cf_conv_w, cf_conv_b, cf_ln_g, cf_ln_b,
                    cf_w_pw2, cf_b_pw2, peer_w_q, peer_sub_keys, peer_u, peer_v,
                    peer_tb=768, peer_eb=1024)
```

```python
import functools
import math

import jax
import jax.numpy as jnp
from jax import lax
from jax.experimental import pallas as pl
from jax.experimental.pallas import tpu as pltpu

N_META = 16
BLOCK_Q = 128
A_HEADS = 8
A_HEAD_DIM = 64
B_CONV = 3
CF_CONV = 31
PEER_HEADS = 8
N_KEYS = 128
PEER_TOPK = 16
EPS = 1e-6

LANES = 128
SUBLANES = 8
VMEM_LIMIT = 48 * 1024 * 1024

_NT = (((1,), (1,)), ((), ()))


def _cparams(*sem):
    return pltpu.CompilerParams(dimension_semantics=sem, vmem_limit_bytes=VMEM_LIMIT)


def _rms(x, g):
    return x * lax.rsqrt(jnp.mean(x * x, axis=-1, keepdims=True) + EPS) * g


def _inproj_kernel(h_ref, g_ref, w_ref, qkv_ref, bb_ref, *, a3):
    y = _rms(h_ref[...], g_ref[...]).astype(jnp.bfloat16)
    p = jnp.dot(y, w_ref[...], preferred_element_type=jnp.float32)
    qkv_ref[...] = p[:, :a3].astype(jnp.bfloat16)
    bb_ref[...] = p[:, a3:]


def _inproj(h, g, w, a3, tm):
    t, d = h.shape
    n = w.shape[1]
    return pl.pallas_call(
        functools.partial(_inproj_kernel, a3=a3),
        grid=(t // tm,),
        in_specs=[pl.BlockSpec((tm, d), lambda i: (i, 0)),
                  pl.BlockSpec((1, d), lambda i: (0, 0)),
                  pl.BlockSpec((d, n), lambda i: (0, 0))],
        out_specs=[pl.BlockSpec((tm, a3), lambda i: (i, 0)),
                   pl.BlockSpec((tm, n - a3), lambda i: (i, 0))],
        out_shape=[jax.ShapeDtypeStruct((t, a3), jnp.bfloat16),
                   jax.ShapeDtypeStruct((t, n - a3), jnp.float32)],
        compiler_params=_cparams("parallel"),
        name="ab_inproj",
    )(h, g, w)


def _glu_kernel(h_ref, g_ref, w_ref, b_ref, z_ref, *, inner):
    y = _rms(h_ref[...], g_ref[...]).astype(jnp.bfloat16)
    p = jnp.dot(y, w_ref[...], preferred_element_type=jnp.float32) + b_ref[...]
    z_ref[...] = p[:, :inner] * jax.nn.sigmoid(p[:, inner:])


def _glu(h, g, w, b, tm):
    t, d = h.shape
    n = w.shape[1]
    inner = n // 2
    return pl.pallas_call(
        functools.partial(_glu_kernel, inner=inner),
        grid=(t // tm,),
        in_specs=[pl.BlockSpec((tm, d), lambda i: (i, 0)),
                  pl.BlockSpec((1, d), lambda i: (0, 0)),
                  pl.BlockSpec((d, n), lambda i: (0, 0)),
                  pl.BlockSpec((1, n), lambda i: (0, 0))],
        out_specs=pl.BlockSpec((tm, inner), lambda i: (i, 0)),
        out_shape=jax.ShapeDtypeStruct((t, inner), jnp.float32),
        compiler_params=_cparams("parallel"),
        name="cf_glu",
    )(h, g, w, b)


F32_EXP_ZERO_BELOW = -104.0
_NEG_INF = float("-inf")


def _attn_kernel(q_ref, k_ref, v_ref, m_ref, o_ref, c_ref, acc_ref, *, bq, pairs, qb):
    g = pl.program_id(1)
    head0 = lax.broadcasted_iota(jnp.int32, (bq, LANES), 1) < A_HEAD_DIM
    row = lax.broadcasted_iota(jnp.int32, (bq, 2 * bq), 0)
    col = lax.broadcasted_iota(jnp.int32, (bq, 2 * bq), 1) & (bq - 1)
    causal = col < row
    mcat = m_ref[...]

    def split_heads(x):
        zero = jnp.zeros_like(x)
        return jnp.concatenate([jnp.where(head0, x, zero), jnp.where(head0, zero, x)], axis=0)

    def tile(step, first):
        chains = [(m, p) for m in range(qb) for p in range(pairs)]
        js = [g * qb + m - step for m in range(qb)]
        krows = [pl.ds(pl.multiple_of(jnp.maximum(j, 0) * bq, bq), bq) for j in js]
        lbs, lhs = [], []
        for m, p in chains:
            lanes = slice(p * LANES, (p + 1) * LANES)
            z = lax.dot_general(q_ref[m * bq:(m + 1) * bq, lanes], split_heads(k_ref[krows[m], lanes]), _NT,
                                preferred_element_type=jnp.float32)
            lb = jnp.minimum(z, 0.0) - jnp.log(1.0 + jnp.exp(-jnp.abs(z)))
            ln = lb - z
            if first:
                ln = jnp.where(causal, ln, 0.0)
            hi = ln.astype(jnp.bfloat16)
            lo = (ln - hi.astype(jnp.float32)).astype(jnp.bfloat16)
            lbs.append(lb)
            for h in range(2):
                lhs.append(jnp.concatenate([hi[:, h * bq:(h + 1) * bq], lo[:, h * bq:(h + 1) * bq]], axis=1))
        st = jnp.dot(jnp.concatenate(lhs, axis=0), mcat, preferred_element_type=jnp.float32)
        live = None
        for n, (m, p) in enumerate(chains):
            lanes = slice(p * LANES, (p + 1) * LANES)
            s0 = st[(2 * n) * bq:(2 * n + 1) * bq]
            s1 = st[(2 * n + 1) * bq:(2 * n + 2) * bq]
            suffix = jnp.concatenate([s0[:, :bq], s1[:, :bq]], axis=1)
            total = jnp.concatenate([s0[:, bq:], s1[:, bq:]], axis=1)
            if first:
                w = jnp.where(causal, jnp.exp(lbs[n] + suffix), 0.0)
                c_new = total
            else:
                c = c_ref[n]
                w = jnp.where(js[m] >= 0, jnp.exp(lbs[n] + suffix + c), 0.0)
                c_new = c + total
            pv = jnp.dot(w.astype(jnp.bfloat16), split_heads(v_ref[krows[m], lanes]),
                         preferred_element_type=jnp.float32)
            acc_ref[n] = pv if first else acc_ref[n] + pv
            c_ref[n] = c_new
            cm = jnp.where(js[m] >= 1, jnp.maximum(c_new[:, :bq], c_new[:, bq:]), _NEG_INF)
            live = cm if live is None else jnp.maximum(live, cm)
        return jnp.max(live)

    live0 = tile(0, True)

    def cond(st):
        return st[1] >= F32_EXP_ZERO_BELOW

    def body(st):
        step = st[0]
        return step + 1, tile(step, False)

    lax.while_loop(cond, body, (1, live0))
    for m in range(qb):
        o_ref[m * bq:(m + 1) * bq, :] = jnp.concatenate(
            [acc_ref[m * pairs + p] for p in range(pairs)], axis=1).astype(o_ref.dtype)


def _attention(qkv, mcat, batch, lp, bq):
    t = qkv.shape[0]
    aw = A_HEADS * A_HEAD_DIM
    pairs = aw // LANES
    nq = lp // bq
    qb = 3 if nq % 3 == 0 else 1
    ng = nq // qb
    return pl.pallas_call(
        functools.partial(_attn_kernel, bq=bq, pairs=pairs, qb=qb),
        grid=(batch, ng),
        in_specs=[pl.BlockSpec((qb * bq, aw), lambda b, i: (b * ng + i, 0)),
                  pl.BlockSpec((lp, aw), lambda b, i: (b, 1)),
                  pl.BlockSpec((lp, aw), lambda b, i: (b, 2)),
                  pl.BlockSpec((2 * bq, 2 * bq), lambda b, i: (0, 0))],
        out_specs=pl.BlockSpec((qb * bq, aw), lambda b, i: (b * ng + i, 0)),
        out_shape=jax.ShapeDtypeStruct((t, aw), jnp.bfloat16),
        scratch_shapes=[pltpu.VMEM((qb * pairs, bq, 2 * bq), jnp.float32),
                        pltpu.VMEM((qb * pairs, bq, LANES), jnp.float32)],
        compiler_params=_cparams("parallel", "arbitrary"),
        name="sb_attention",
    )(qkv, qkv, qkv, mcat)


def _about_kernel(a_ref, bb_ref, halo_ref, cw_ref, w_ref, h_ref, o_ref, *, nb, bw):
    i = pl.program_id(0)
    bb = bb_ref[...]
    gate_b, gate_c, hb = bb[:, :bw], bb[:, bw:2 * bw], bb[:, 2 * bw:]
    u = gate_c * hb
    hal = halo_ref[...]
    uh = hal[:, bw:2 * bw] * hal[:, 2 * bw:]
    uh = jnp.where(i % nb == 0, 0.0, uh)
    ucat = jnp.concatenate([uh, u], axis=0)
    tm = u.shape[0]
    cw = cw_ref[...]
    conv = u * cw[B_CONV - 1:B_CONV, :]
    for s in range(1, B_CONV):
        us = pltpu.roll(ucat, s, 0)[SUBLANES:SUBLANES + tm, :]
        conv = conv + us * cw[B_CONV - 1 - s:B_CONV - s, :]
    b_out = (gate_b * conv).astype(jnp.bfloat16)
    w = w_ref[...]
    aw = a_ref.shape[1]
    o_ref[...] = (h_ref[...]
                  + jnp.dot(a_ref[...], w[:aw, :], preferred_element_type=jnp.float32)
                  + jnp.dot(b_out, w[aw:, :], preferred_element_type=jnp.float32))


def _about(a_out, bb, conv_w, w_out, h, lp, tm):
    t, d = h.shape
    aw = a_out.shape[1]
    bw = conv_w.shape[1]
    nb = lp // tm
    hb = tm // SUBLANES
    return pl.pallas_call(
        functools.partial(_about_kernel, nb=nb, bw=bw),
        grid=(t // tm,),
        in_specs=[pl.BlockSpec((tm, aw), lambda i: (i, 0)),
                  pl.BlockSpec((tm, 3 * bw), lambda i: (i, 0)),
                  pl.BlockSpec((SUBLANES, 3 * bw), lambda i: (jnp.maximum(i * hb - 1, 0), 0)),
                  pl.BlockSpec((B_CONV, bw), lambda i: (0, 0)),
                  pl.BlockSpec((aw + bw, d), lambda i: (0, 0)),
                  pl.BlockSpec((tm, d), lambda i: (i, 0))],
        out_specs=pl.BlockSpec((tm, d), lambda i: (i, 0)),
        out_shape=jax.ShapeDtypeStruct((t, d), jnp.float32),
        compiler_params=_cparams("parallel"),
        name="ab_out",
    )(a_out, bb, bb, conv_w, w_out, h)


CF_HALO = 32


def _cfout_kernel(z_ref, halo_ref, cw_ref, cb_ref, lg_ref, lb_ref, w_ref, b_ref, h_ref, o_ref,
                  zs_ref, cv_ref, zp_ref, *, nb, rc):
    i = pl.program_id(0)
    tm, c = z_ref.shape
    zs_ref[0:CF_HALO, :] = jnp.where(i % nb == 0, 0.0, halo_ref[...])
    zs_ref[CF_HALO:, :] = z_ref[...]

    span = tm + CF_HALO - SUBLANES

    def conv_cols(j, _):
        cols = pl.ds(pl.multiple_of(j * LANES, LANES), LANES)
        cw = cw_ref[:, cols]
        cb = cb_ref[:, cols]
        for ph in range(1, SUBLANES):
            zp_ref[ph, 0:span, :] = zs_ref[ph:ph + span, cols]
        for r in range(0, tm, rc):
            acc = jnp.zeros((rc, LANES), jnp.float32) + cb
            for k in range(CF_CONV):
                off = r + CF_HALO - (CF_CONV - 1) + k
                ph = off % SUBLANES
                tap = zs_ref[off:off + rc, cols] if ph == 0 else zp_ref[ph, off - ph:off - ph + rc, :]
                acc = acc + tap * cw[k:k + 1, :]
            cv_ref[r:r + rc, cols] = acc
        return 0

    lax.fori_loop(0, c // LANES, conv_cols, 0)
    acc = cv_ref[...]
    mu = jnp.mean(acc, axis=-1, keepdims=True)
    xc = acc - mu
    var = jnp.mean(xc * xc, axis=-1, keepdims=True)
    y = xc * lax.rsqrt(var + EPS) * lg_ref[...] + lb_ref[...]
    y = y * jax.nn.sigmoid(y)
    o_ref[...] = (h_ref[...] + b_ref[...]
                  + jnp.dot(y.astype(jnp.bfloat16), w_ref[...], preferred_element_type=jnp.float32))


def _cfout(z, conv_w, conv_b, ln_g, ln_b, w2, b2, h, lp, tm):
    t, d = h.shape
    c = z.shape[1]
    nb = lp // tm
    hb = tm // CF_HALO
    vec = lambda n: pl.BlockSpec((1, n), lambda i: (0, 0))
    return pl.pallas_call(
        functools.partial(_cfout_kernel, nb=nb, rc=BLOCK_Q),
        grid=(t // tm,),
        in_specs=[pl.BlockSpec((tm, c), lambda i: (i, 0)),
                  pl.BlockSpec((CF_HALO, c), lambda i: (jnp.maximum(i * hb - 1, 0), 0)),
                  pl.BlockSpec((CF_CONV, c), lambda i: (0, 0)),
                  vec(c), vec(c), vec(c),
                  pl.BlockSpec((c, d), lambda i: (0, 0)),
                  vec(d),
                  pl.BlockSpec((tm, d), lambda i: (i, 0))],
        out_specs=pl.BlockSpec((tm, d), lambda i: (i, 0)),
        out_shape=jax.ShapeDtypeStruct((t, d), jnp.float32),
        scratch_shapes=[pltpu.VMEM((tm + CF_HALO, c), jnp.float32),
                        pltpu.VMEM((tm, c), jnp.float32),
                        pltpu.VMEM((SUBLANES, tm + CF_HALO, LANES), jnp.float32)],
        compiler_params=_cparams("parallel"),
        name="cf_out",
    )(z, z, conv_w, conv_b, ln_g, ln_b, w2, b2, h)


_NEG = float("-inf")
_BIGKEY = float(2 ** 30)
G_ROW_PAD = 4
G_ROW_STRIDE = N_KEYS + G_ROW_PAD
TOPK_ILP = 8


def _extract_topk(x, key, k):
    vals, keys = [], []
    for _ in range(k):
        m = jnp.max(x, axis=0, keepdims=True)
        km = jnp.min(jnp.where(x == m, key, _BIGKEY), axis=0, keepdims=True)
        x = jnp.where(key == km, _NEG, x)
        vals.append(m)
        keys.append(km)
    return jnp.concatenate(vals, axis=0), jnp.concatenate(keys, axis=0)


def _batcher_network(n):
    def merge(lo, hi, r):
        step = 2 * r
        if step < hi - lo:
            yield from merge(lo, hi, step)
            yield from merge(lo + r, hi, step)
            yield from ((i, i + r) for i in range(lo + r, hi - r, step))
        else:
            yield (lo, lo + r)

    def sort(lo, hi):
        if hi - lo >= 1:
            mid = lo + (hi - lo) // 2
            yield from sort(lo, mid)
            yield from sort(mid + 1, hi)
            yield from merge(lo, hi, 1)

    return tuple(sort(0, n - 1))


def _sublane_all(op, x):
    shift = SUBLANES // 2
    while shift:
        x = op(x, pltpu.roll(x, shift, 0))
        shift //= 2
    return x


def _topk_distinct(x, key, k):
    depth = x.shape[0] // SUBLANES
    vs = [x[d * SUBLANES:(d + 1) * SUBLANES] for d in range(depth)]
    ks = [key[d * SUBLANES:(d + 1) * SUBLANES] for d in range(depth)]
    for i, j in _batcher_network(depth):
        swap = vs[j] > vs[i]
        vs[i], vs[j] = jnp.maximum(vs[i], vs[j]), jnp.minimum(vs[i], vs[j])
        ks[i], ks[j] = jnp.where(swap, ks[j], ks[i]), jnp.where(swap, ks[i], ks[j])
    vals, keys = [], []
    tied = jnp.zeros(vs[0].shape, jnp.float32)
    prev = None
    for r in range(k + 1):
        m = _sublane_all(jnp.maximum, vs[0])
        if prev is not None:
            tied = jnp.maximum(tied, jnp.where(m >= prev, 1.0, 0.0))
        prev = m
        if r == k:
            break
        km = _sublane_all(jnp.minimum, jnp.where(vs[0] == m, ks[0], _BIGKEY))
        win = ks[0] == km
        vals.append(m[0:1])
        keys.append(km[0:1])
        for d in range(min(depth, k - r)):
            if d + 1 < depth:
                vs[d] = jnp.where(win, vs[d + 1], vs[d])
                ks[d] = jnp.where(win, ks[d + 1], ks[d])
            else:
                vs[d] = jnp.where(win, _NEG, vs[d])
    return jnp.concatenate(vals, axis=0), jnp.concatenate(keys, axis=0), tied


def _route_kernel(h_ref, g_ref, wq_ref, sk_ref, gm_ref, s_ref, val_ref, idx_ref, sel_ref, selt_ref,
                  gs_ref):
    tr = h_ref.shape[0]
    nk = N_KEYS
    topk = PEER_TOPK
    nhp = 2 * PEER_HEADS
    y = _rms(h_ref[...], g_ref[...]).astype(jnp.bfloat16)
    q = jnp.dot(y, wq_ref[...], preferred_element_type=jnp.float32).astype(jnp.bfloat16)
    for hp in range(nhp):
        s_ref[hp] = lax.dot_general(sk_ref[hp % 2], q[:, hp * nk:(hp + 1) * nk], _NT,
                                    preferred_element_type=jnp.float32)

    key_iota = lax.broadcasted_iota(jnp.int32, (nk, tr), 0).astype(jnp.float32)

    def level1(n, tied):
        for s in range(TOPK_ILP):
            v, ix, td = _topk_distinct(s_ref[TOPK_ILP * n + s], key_iota, topk)
            val_ref[TOPK_ILP * n + s] = v
            idx_ref[TOPK_ILP * n + s] = ix
            tied = jnp.maximum(tied, td)
        return tied

    tied = lax.fori_loop(0, nhp // TOPK_ILP, level1, jnp.zeros((SUBLANES, tr), jnp.float32))

    @pl.when(jnp.max(tied) > 0.0)
    def _():
        def level1_ties(n, _):
            v, ix = _extract_topk(s_ref[n], key_iota, topk)
            val_ref[n] = v
            idx_ref[n] = ix
            return 0

        lax.fori_loop(0, nhp, level1_ties, 0)

    half = topk // 2
    sub8 = lax.broadcasted_iota(jnp.int32, (half, tr), 0).astype(jnp.float32)
    sub16 = lax.broadcasted_iota(jnp.int32, (topk, tr), 0).astype(jnp.float32)

    def level2(n, _):
        for s in range(TOPK_ILP):
            hd = TOPK_ILP * n + s
            v0, v1 = val_ref[2 * hd], val_ref[2 * hd + 1]
            i0, i1 = idx_ref[2 * hd], idx_ref[2 * hd + 1]
            code = lambda pos, a, b: pos * float(nk * nk) + a * float(nk) + b
            cand = [v0[0:1] + v1]
            keys = [code(sub16, i0[0:1], i1)]
            for k0 in range(1, half):
                cand.append(v0[k0:k0 + 1] + v1[:half])
                keys.append(code(k0 * topk + sub8, i0[k0:k0 + 1], i1[:half]))
            cand.append(v0[half:] + v1[0:1])
            keys.append(code((half + sub8) * topk, i0[half:], i1[0:1]))
            tv, tk = _extract_topk(jnp.concatenate(cand, axis=0), jnp.concatenate(keys, axis=0), topk)
            e = jnp.exp(tv - tv[0:1])
            gate = e / jnp.sum(e, axis=0, keepdims=True)
            expert = tk.astype(jnp.int32) & (nk * nk - 1)
            rows = pl.ds(pl.multiple_of(hd * topk, topk), topk)
            sel_ref[0, rows, :] = gate
            sel_ref[1, rows, :] = (expert >> 7).astype(jnp.float32)
            sel_ref[2, rows, :] = (expert & (nk - 1)).astype(jnp.float32)
        return 0

    lax.fori_loop(0, PEER_HEADS // TOPK_ILP, level2, 0)

    for a in range(3):
        selt_ref[a] = sel_ref[a].T

    mrows = nk + 2 * G_ROW_PAD
    sub_l = lax.broadcasted_iota(jnp.int32, (mrows, LANES), 0).astype(jnp.float32)
    sub_r = lax.broadcasted_iota(jnp.int32, (nk, LANES), 0).astype(jnp.float32)
    zblk = jnp.zeros((nk, LANES), jnp.bfloat16)

    def token_pair(tp, _):
        lhs, rhs = [], []
        for s in range(2):
            t = 2 * tp + s
            gt = selt_ref[0, pl.ds(t, 1), :]
            it0 = selt_ref[1, pl.ds(t, 1), :] + float(s * G_ROW_PAD)
            it1 = selt_ref[2, pl.ds(t, 1), :]
            lhs.append(jnp.where(sub_l == it0, 1.0, 0.0).astype(jnp.bfloat16))
            rhs.append(jnp.where(sub_r == it1, gt, 0.0).astype(jnp.bfloat16))
        lhs2 = jnp.concatenate(lhs, axis=1)
        rhs2 = jnp.concatenate([jnp.concatenate([rhs[0], zblk], axis=1),
                                jnp.concatenate([zblk, rhs[1]], axis=1)], axis=0)
        g2 = lax.dot_general(lhs2, rhs2, _NT, preferred_element_type=jnp.float32)
        base = pl.multiple_of(2 * tp * G_ROW_STRIDE, SUBLANES)
        gs_ref[pl.ds(base, nk), :] = g2[:nk, :nk]
        gs_ref[pl.ds(base + nk, mrows), :] = g2[:, nk:]
        return 0

    lax.fori_loop(0, tr // 2, token_pair, 0, unroll=32)

    for i in range(nk):
        gm_ref[:, i * nk:(i + 1) * nk] = gs_ref[pl.ds(i, tr, stride=G_ROW_STRIDE), :].astype(gm_ref.dtype)


def _route(h, g, wq, sk):
    t, d = h.shape
    nq = wq.shape[1]
    nk = N_KEYS
    tr = LANES
    nhp = 2 * PEER_HEADS
    hk = PEER_HEADS * PEER_TOPK
    return pl.pallas_call(
        _route_kernel,
        grid=(t // tr,),
        in_specs=[pl.BlockSpec((tr, d), lambda i: (i, 0)),
                  pl.BlockSpec((1, d), lambda i: (0, 0)),
                  pl.BlockSpec((d, nq), lambda i: (0, 0)),
                  pl.BlockSpec((2, nk, nk), lambda i: (0, 0, 0))],
        out_specs=pl.BlockSpec((tr, nk * nk), lambda i: (i, 0)),
        out_shape=jax.ShapeDtypeStruct((t, nk * nk), jnp.bfloat16),
        scratch_shapes=[pltpu.VMEM((nhp, nk, tr), jnp.float32),
                        pltpu.VMEM((nhp, PEER_TOPK, tr), jnp.float32),
                        pltpu.VMEM((nhp, PEER_TOPK, tr), jnp.float32),
                        pltpu.VMEM((3, hk, tr), jnp.float32),
                        pltpu.VMEM((3, tr, hk), jnp.float32),
                        pltpu.VMEM((tr * G_ROW_STRIDE, nk), jnp.float32)],
        compiler_params=_cparams("parallel"),
        name="peer_route",
    )(h, g, wq, sk)


def _peer_kernel(h_ref, g_ref, gm_ref, u_ref, v_ref, fg_ref, o_ref, y_ref, acc_ref, *, final_norm):
    e = pl.program_id(1)

    @pl.when(e == 0)
    def _():
        y_ref[...] = _rms(h_ref[...], g_ref[...]).astype(jnp.bfloat16)
        acc_ref[...] = jnp.zeros_like(acc_ref)

    hid = lax.dot_general(y_ref[...], u_ref[...], _NT, preferred_element_type=jnp.float32)
    act = 0.5 * hid * (1.0 + lax.erf(hid * (1.0 / math.sqrt(2.0))))
    a = (gm_ref[...].astype(jnp.float32) * act).astype(jnp.bfloat16)
    acc_ref[...] += jnp.dot(a, v_ref[...], preferred_element_type=jnp.float32)

    @pl.when(e == pl.num_programs(1) - 1)
    def _():
        out = h_ref[...] + acc_ref[...]
        if final_norm:
            out = _rms(out, fg_ref[...])
        o_ref[...] = out


def _peer(h, g, gm, u, v, layer, fg, tb, eb, final_norm):
    t, d = h.shape
    ne = v.shape[1]
    return pl.pallas_call(
        functools.partial(_peer_kernel, final_norm=final_norm),
        grid=(t // tb, ne // eb),
        in_specs=[pl.BlockSpec((tb, d), lambda i, e: (i, 0)),
                  pl.BlockSpec((1, d), lambda i, e: (0, 0)),
                  pl.BlockSpec((tb, eb), lambda i, e: (i, e)),
                  pl.BlockSpec((None, eb, d), lambda i, e: (layer, e, 0)),
                  pl.BlockSpec((None, eb, d), lambda i, e: (layer, e, 0)),
                  pl.BlockSpec((1, d), lambda i, e: (0, 0))],
        out_specs=pl.BlockSpec((tb, d), lambda i, e: (i, 0)),
        out_shape=jax.ShapeDtypeStruct((t, d), jnp.float32),
        scratch_shapes=[pltpu.VMEM((tb, d), jnp.bfloat16),
                        pltpu.VMEM((tb, d), jnp.float32)],
        compiler_params=_cparams("parallel", "arbitrary"),
        name="peer_experts",
    )(h, g, gm, u, v, fg)


def _seq_block(lp):
    return 3 * BLOCK_Q if lp % (3 * BLOCK_Q) == 0 else BLOCK_Q


def _forward(x, meta_tokens, mix_norm_g, ffn_norm_g, final_norm_g, ab_w_in, ab_conv_w,
             ab_w_out, cf_w_pw1, cf_b_pw1, cf_conv_w, cf_conv_b, cf_ln_g, cf_ln_b,
             cf_w_pw2, cf_b_pw2, peer_w_q, peer_sub_keys, peer_u, peer_v,
             peer_tb, peer_eb):
    bsz, s, d = x.shape
    l = s + N_META
    lp = -(-l // BLOCK_Q) * BLOCK_Q
    t = bsz * lp
    depth = mix_norm_g.shape[0]
    seq_blk = _seq_block(lp)
    bf = jnp.bfloat16
    row = lambda a: a.reshape(1, -1)

    h = jnp.concatenate([
        jnp.broadcast_to(meta_tokens.astype(x.dtype)[None], (bsz, N_META, d)),
        x,
        jnp.zeros((bsz, lp - l, d), x.dtype)], axis=1).reshape(t, d)

    a_width = A_HEADS * A_HEAD_DIM
    tri = (jnp.arange(BLOCK_Q)[:, None] > jnp.arange(BLOCK_Q)[None, :]).astype(bf)
    mhalf = jnp.concatenate([tri, jnp.ones((BLOCK_Q, BLOCK_Q), bf)], axis=1)
    mcat = jnp.concatenate([mhalf, mhalf], axis=0)
    qscale = jnp.concatenate([jnp.full((a_width,), 1.0 / math.sqrt(A_HEAD_DIM), jnp.float32),
                              jnp.ones((ab_w_in.shape[2] - a_width,), jnp.float32)])

    u_bf, v_bf = peer_u.astype(bf), peer_v.astype(bf)
    for i in range(depth):
        j = i // 2
        if i % 2 == 0:
            w_in = (ab_w_in[j] * qscale[None, :]).astype(bf)
            qkv, bb = _inproj(h, row(mix_norm_g[i]), w_in, 3 * a_width, seq_blk)
            a_out = _attention(qkv, mcat, bsz, lp, BLOCK_Q)
            h = _about(a_out, bb, ab_conv_w[j], ab_w_out[j].astype(bf), h, lp, seq_blk)
        else:
            z = _glu(h, row(mix_norm_g[i]), cf_w_pw1[j].astype(bf), row(cf_b_pw1[j]), seq_blk)
            h = _cfout(z, cf_conv_w[j], row(cf_conv_b[j]), row(cf_ln_g[j]), row(cf_ln_b[j]),
                       cf_w_pw2[j].astype(bf), row(cf_b_pw2[j]), h, lp, seq_blk)
        gm = _route(h, row(ffn_norm_g[i]), peer_w_q[i].astype(bf), peer_sub_keys[i].astype(bf))
        h = _peer(h, row(ffn_norm_g[i]), gm, u_bf, v_bf, i, row(final_norm_g),
                  peer_tb, peer_eb, final_norm=(i == depth - 1))
    return h.reshape(bsz, lp, d)[:, N_META:N_META + s, :]


def kernel(x, meta_tokens, mix_norm_g, ffn_norm_g, final_norm_g, ab_w_in, ab_conv_w, ab_w_out, cf_w_pw1, cf_b_pw1, cf_conv_w, cf_conv_b, cf_ln_g, cf_ln_b, cf_w_pw2, cf_b_pw2, peer_w_q, peer_sub_keys, peer_u, peer_v):
    return _forward(x, meta_tokens, mix_norm_g, ffn_norm_g, final_norm_g, ab_w_in, ab_conv_w,
                    ab_w_out, cf_w_pw1, cf_b_pw1, cf_conv_w, cf_conv_b, cf_ln_g, cf_ln_b,
                    cf_w_pw2, cf_b_pw2, peer_w_q, peer_sub_keys, peer_u, peer_v,
                    peer_tb=768, peer_eb=1024)
```

```python
import functools
import math

import jax
import jax.numpy as jnp
from jax import lax
from jax.experimental import pallas as pl
from jax.experimental.pallas import tpu as pltpu

N_META = 16
BLOCK_Q = 128
A_HEADS = 8
A_HEAD_DIM = 64
B_CONV = 3
CF_CONV = 31
PEER_HEADS = 8
N_KEYS = 128
PEER_TOPK = 16
EPS = 1e-6

LANES = 128
SUBLANES = 8
VMEM_LIMIT = 48 * 1024 * 1024

_NT = (((1,), (1,)), ((), ()))


def _cparams(*sem):
    return pltpu.CompilerParams(dimension_semantics=sem, vmem_limit_bytes=VMEM_LIMIT)


def _rms(x, g):
    return x * lax.rsqrt(jnp.mean(x * x, axis=-1, keepdims=True) + EPS) * g


def _inproj_kernel(h_ref, g_ref, w_ref, qkv_ref, bb_ref, *, a3):
    y = _rms(h_ref[...], g_ref[...]).astype(jnp.bfloat16)
    p = jnp.dot(y, w_ref[...], preferred_element_type=jnp.float32)
    qkv_ref[...] = p[:, :a3].astype(jnp.bfloat16)
    bb_ref[...] = p[:, a3:]


def _inproj(h, g, w, a3, tm):
    t, d = h.shape
    n = w.shape[1]
    return pl.pallas_call(
        functools.partial(_inproj_kernel, a3=a3),
        grid=(t // tm,),
        in_specs=[pl.BlockSpec((tm, d), lambda i: (i, 0)),
                  pl.BlockSpec((1, d), lambda i: (0, 0)),
                  pl.BlockSpec((d, n), lambda i: (0, 0))],
        out_specs=[pl.BlockSpec((tm, a3), lambda i: (i, 0)),
                   pl.BlockSpec((tm, n - a3), lambda i: (i, 0))],
        out_shape=[jax.ShapeDtypeStruct((t, a3), jnp.bfloat16),
                   jax.ShapeDtypeStruct((t, n - a3), jnp.float32)],
        compiler_params=_cparams("parallel"),
        name="ab_inproj",
    )(h, g, w)


def _glu_kernel(h_ref, g_ref, w_ref, b_ref, z_ref, *, inner):
    y = _rms(h_ref[...], g_ref[...]).astype(jnp.bfloat16)
    p = jnp.dot(y, w_ref[...], preferred_element_type=jnp.float32) + b_ref[...]
    z_ref[...] = p[:, :inner] * jax.nn.sigmoid(p[:, inner:])


def _glu(h, g, w, b, tm):
    t, d = h.shape
    n = w.shape[1]
    inner = n // 2
    return pl.pallas_call(
        functools.partial(_glu_kernel, inner=inner),
        grid=(t // tm,),
        in_specs=[pl.BlockSpec((tm, d), lambda i: (i, 0)),
                  pl.BlockSpec((1, d), lambda i: (0, 0)),
                  pl.BlockSpec((d, n), lambda i: (0, 0)),
                  pl.BlockSpec((1, n), lambda i: (0, 0))],
        out_specs=pl.BlockSpec((tm, inner), lambda i: (i, 0)),
        out_shape=jax.ShapeDtypeStruct((t, inner), jnp.float32),
        compiler_params=_cparams("parallel"),
        name="cf_glu",
    )(h, g, w, b)


F32_EXP_ZERO_BELOW = -104.0
_NEG_INF = float("-inf")


def _attn_kernel(q_ref, k_ref, v_ref, m_ref, o_ref, c_ref, acc_ref, *, bq, pairs, qb):
    g = pl.program_id(1)
    head0 = lax.broadcasted_iota(jnp.int32, (bq, LANES), 1) < A_HEAD_DIM
    row = lax.broadcasted_iota(jnp.int32, (bq, 2 * bq), 0)
    col = lax.broadcasted_iota(jnp.int32, (bq, 2 * bq), 1) & (bq - 1)
    causal = col < row
    mcat = m_ref[...]

    def split_heads(x):
        zero = jnp.zeros_like(x)
        return jnp.concatenate([jnp.where(head0, x, zero), jnp.where(head0, zero, x)], axis=0)

    def tile(step, first):
        chains = [(m, p) for m in range(qb) for p in range(pairs)]
        js = [g * qb + m - step for m in range(qb)]
        krows = [pl.ds(pl.multiple_of(jnp.maximum(j, 0) * bq, bq), bq) for j in js]
        lbs, lhs = [], []
        for m, p in chains:
            lanes = slice(p * LANES, (p + 1) * LANES)
            z = lax.dot_general(q_ref[m * bq:(m + 1) * bq, lanes], split_heads(k_ref[krows[m], lanes]), _NT,
                                preferred_element_type=jnp.float32)
            lb = jnp.minimum(z, 0.0) - jnp.log(1.0 + jnp.exp(-jnp.abs(z)))
            ln = lb - z
            if first:
                ln = jnp.where(causal, ln, 0.0)
            hi = ln.astype(jnp.bfloat16)
            lo = (ln - hi.astype(jnp.float32)).astype(jnp.bfloat16)
            lbs.append(lb)
            for h in range(2):
                lhs.append(jnp.concatenate([hi[:, h * bq:(h + 1) * bq], lo[:, h * bq:(h + 1) * bq]], axis=1))
        st = jnp.dot(jnp.concatenate(lhs, axis=0), mcat, preferred_element_type=jnp.float32)
        live = None
        for n, (m, p) in enumerate(chains):
            lanes = slice(p * LANES, (p + 1) * LANES)
            s0 = st[(2 * n) * bq:(2 * n + 1) * bq]
            s1 = st[(2 * n + 1) * bq:(2 * n + 2) * bq]
            suffix = jnp.concatenate([s0[:, :bq], s1[:, :bq]], axis=1)
            total = jnp.concatenate([s0[:, bq:], s1[:, bq:]], axis=1)
            if first:
                w = jnp.where(causal, jnp.exp(lbs[n] + suffix), 0.0)
                c_new = total
            else:
                c = c_ref[n]
                w = jnp.where(js[m] >= 0, jnp.exp(lbs[n] + suffix + c), 0.0)
                c_new = c + total
            pv = jnp.dot(w.astype(jnp.bfloat16), split_heads(v_ref[krows[m], lanes]),
                         preferred_element_type=jnp.float32)
            acc_ref[n] = pv if first else acc_ref[n] + pv
            c_ref[n] = c_new
            cm = jnp.where(js[m] >= 1, jnp.maximum(c_new[:, :bq], c_new[:, bq:]), _NEG_INF)
            live = cm if live is None else jnp.maximum(live, cm)
        return jnp.max(live)

    live0 = tile(0, True)

    def cond(st):
        return st[1] >= F32_EXP_ZERO_BELOW

    def body(st):
        step = st[0]
        return step + 1, tile(step, False)

    lax.while_loop(cond, body, (1, live0))
    for m in range(qb):
        o_ref[m * bq:(m + 1) * bq, :] = jnp.concatenate(
            [acc_ref[m * pairs + p] for p in range(pairs)], axis=1).astype(o_ref.dtype)


def _attention(qkv, mcat, batch, lp, bq):
    t = qkv.shape[0]
    aw = A_HEADS * A_HEAD_DIM
    pairs = aw // LANES
    nq = lp // bq
    qb = 3 if nq % 3 == 0 else 1
    ng = nq // qb
    return pl.pallas_call(
        functools.partial(_attn_kernel, bq=bq, pairs=pairs, qb=qb),
        grid=(batch, ng),
        in_specs=[pl.BlockSpec((qb * bq, aw), lambda b, i: (b * ng + i, 0)),
                  pl.BlockSpec((lp, aw), lambda b, i: (b, 1)),
                  pl.BlockSpec((lp, aw), lambda b, i: (b, 2)),
                  pl.BlockSpec((2 * bq, 2 * bq), lambda b, i: (0, 0))],
        out_specs=pl.BlockSpec((qb * bq, aw), lambda b, i: (b * ng + i, 0)),
        out_shape=jax.ShapeDtypeStruct((t, aw), jnp.bfloat16),
        scratch_shapes=[pltpu.VMEM((qb * pairs, bq, 2 * bq), jnp.float32),
                        pltpu.VMEM((qb * pairs, bq, LANES), jnp.float32)],
        compiler_params=_cparams("parallel", "arbitrary"),
        name="sb_attention",
    )(qkv, qkv, qkv, mcat)


def _about_kernel(a_ref, bb_ref, halo_ref, cw_ref, w_ref, h_ref, o_ref, *, nb, bw):
    i = pl.program_id(0)
    bb = bb_ref[...]
    gate_b, gate_c, hb = bb[:, :bw], bb[:, bw:2 * bw], bb[:, 2 * bw:]
    u = gate_c * hb
    hal = halo_ref[...]
    uh = hal[:, bw:2 * bw] * hal[:, 2 * bw:]
    uh = jnp.where(i % nb == 0, 0.0, uh)
    ucat = jnp.concatenate([uh, u], axis=0)
    tm = u.shape[0]
    cw = cw_ref[...]
    conv = u * cw[B_CONV - 1:B_CONV, :]
    for s in range(1, B_CONV):
        us = pltpu.roll(ucat, s, 0)[SUBLANES:SUBLANES + tm, :]
        conv = conv + us * cw[B_CONV - 1 - s:B_CONV - s, :]
    b_out = (gate_b * conv).astype(jnp.bfloat16)
    w = w_ref[...]
    aw = a_ref.shape[1]
    o_ref[...] = (h_ref[...]
                  + jnp.dot(a_ref[...], w[:aw, :], preferred_element_type=jnp.float32)
                  + jnp.dot(b_out, w[aw:, :], preferred_element_type=jnp.float32))


def _about(a_out, bb, conv_w, w_out, h, lp, tm):
    t, d = h.shape
    aw = a_out.shape[1]
    bw = conv_w.shape[1]
    nb = lp // tm
    hb = tm // SUBLANES
    return pl.pallas_call(
        functools.partial(_about_kernel, nb=nb, bw=bw),
        grid=(t // tm,),
        in_specs=[pl.BlockSpec((tm, aw), lambda i: (i, 0)),
                  pl.BlockSpec((tm, 3 * bw), lambda i: (i, 0)),
                  pl.BlockSpec((SUBLANES, 3 * bw), lambda i: (jnp.maximum(i * hb - 1, 0), 0)),
                  pl.BlockSpec((B_CONV, bw), lambda i: (0, 0)),
                  pl.BlockSpec((aw + bw, d), lambda i: (0, 0)),
                  pl.BlockSpec((tm, d), lambda i: (i, 0))],
        out_specs=pl.BlockSpec((tm, d), lambda i: (i, 0)),
        out_shape=jax.ShapeDtypeStruct((t, d), jnp.float32),
        compiler_params=_cparams("parallel"),
        name="ab_out",
    )(a_out, bb, bb, conv_w, w_out, h)


CF_HALO = 32


def _cfout_kernel(z_ref, halo_ref, cw_ref, cb_ref, lg_ref, lb_ref, w_ref, b_ref, h_ref, o_ref,
                  zs_ref, cv_ref, zp_ref, *, nb, rc):
    i = pl.program_id(0)
    tm, c = z_ref.shape
    zs_ref[0:CF_HALO, :] = jnp.where(i % nb == 0, 0.0, halo_ref[...])
    zs_ref[CF_HALO:, :] = z_ref[...]

    span = tm + CF_HALO - SUBLANES

    def conv_cols(j, _):
        cols = pl.ds(pl.multiple_of(j * LANES, LANES), LANES)
        cw = cw_ref[:, cols]
        cb = cb_ref[:, cols]
        for ph in range(1, SUBLANES):
            zp_ref[ph, 0:span, :] = zs_ref[ph:ph + span, cols]
        for r in range(0, tm, rc):
            acc = jnp.zeros((rc, LANES), jnp.float32) + cb
            for k in range(CF_CONV):
                off = r + CF_HALO - (CF_CONV - 1) + k
                ph = off % SUBLANES
                tap = zs_ref[off:off + rc, cols] if ph == 0 else zp_ref[ph, off - ph:off - ph + rc, :]
                acc = acc + tap * cw[k:k + 1, :]
            cv_ref[r:r + rc, cols] = acc
        return 0

    lax.fori_loop(0, c // LANES, conv_cols, 0)
    acc = cv_ref[...]
    mu = jnp.mean(acc, axis=-1, keepdims=True)
    xc = acc - mu
    var = jnp.mean(xc * xc, axis=-1, keepdims=True)
    y = xc * lax.rsqrt(var + EPS) * lg_ref[...] + lb_ref[...]
    y = y * jax.nn.sigmoid(y)
    o_ref[...] = (h_ref[...] + b_ref[...]
                  + jnp.dot(y.astype(jnp.bfloat16), w_ref[...], preferred_element_type=jnp.float32))


def _cfout(z, conv_w, conv_b, ln_g, ln_b, w2, b2, h, lp, tm):
    t, d = h.shape
    c = z.shape[1]
    nb = lp // tm
    hb = tm // CF_HALO
    vec = lambda n: pl.BlockSpec((1, n), lambda i: (0, 0))
    return pl.pallas_call(
        functools.partial(_cfout_kernel, nb=nb, rc=BLOCK_Q),
        grid=(t // tm,),
        in_specs=[pl.BlockSpec((tm, c), lambda i: (i, 0)),
                  pl.BlockSpec((CF_HALO, c), lambda i: (jnp.maximum(i * hb - 1, 0), 0)),
                  pl.BlockSpec((CF_CONV, c), lambda i: (0, 0)),
                  vec(c), vec(c), vec(c),
                  pl.BlockSpec((c, d), lambda i: (0, 0)),
                  vec(d),
                  pl.BlockSpec((tm, d), lambda i: (i, 0))],
        out_specs=pl.BlockSpec((tm, d), lambda i: (i, 0)),
        out_shape=jax.ShapeDtypeStruct((t, d), jnp.float32),
        scratch_shapes=[pltpu.VMEM((tm + CF_HALO, c), jnp.float32),
                        pltpu.VMEM((tm, c), jnp.float32),
                        pltpu.VMEM((SUBLANES, tm + CF_HALO, LANES), jnp.float32)],
        compiler_params=_cparams("parallel"),
        name="cf_out",
    )(z, z, conv_w, conv_b, ln_g, ln_b, w2, b2, h)


_NEG = float("-inf")
_BIGKEY = float(2 ** 30)
G_ROW_PAD = 4
G_ROW_STRIDE = N_KEYS + G_ROW_PAD
TOPK_ILP = 4


def _extract_topk(x, key, k):
    vals, keys = [], []
    for _ in range(k):
        m = jnp.max(x, axis=0, keepdims=True)
        km = jnp.min(jnp.where(x == m, key, _BIGKEY), axis=0, keepdims=True)
        x = jnp.where(key == km, _NEG, x)
        vals.append(m)
        keys.append(km)
    return jnp.concatenate(vals, axis=0), jnp.concatenate(keys, axis=0)


def _batcher_network(n):
    def merge(lo, hi, r):
        step = 2 * r
        if step < hi - lo:
            yield from merge(lo, hi, step)
            yield from merge(lo + r, hi, step)
            yield from ((i, i + r) for i in range(lo + r, hi - r, step))
        else:
            yield (lo, lo + r)

    def sort(lo, hi):
        if hi - lo >= 1:
            mid = lo + (hi - lo) // 2
            yield from sort(lo, mid)
            yield from sort(mid + 1, hi)
            yield from merge(lo, hi, 1)

    return tuple(sort(0, n - 1))


def _sublane_all(op, x):
    shift = SUBLANES // 2
    while shift:
        x = op(x, pltpu.roll(x, shift, 0))
        shift //= 2
    return x


def _topk_distinct(x, key, k):
    depth = x.shape[0] // SUBLANES
    vs = [x[d * SUBLANES:(d + 1) * SUBLANES] for d in range(depth)]
    ks = [key[d * SUBLANES:(d + 1) * SUBLANES] for d in range(depth)]
    for i, j in _batcher_network(depth):
        swap = vs[j] > vs[i]
        vs[i], vs[j] = jnp.maximum(vs[i], vs[j]), jnp.minimum(vs[i], vs[j])
        ks[i], ks[j] = jnp.where(swap, ks[j], ks[i]), jnp.where(swap, ks[i], ks[j])
    vals, keys = [], []
    tied = jnp.zeros(vs[0].shape, jnp.float32)
    prev = None
    for r in range(k + 1):
        m = _sublane_all(jnp.maximum, vs[0])
        if prev is not None:
            tied = jnp.maximum(tied, jnp.where(m >= prev, 1.0, 0.0))
        prev = m
        if r == k:
            break
        km = _sublane_all(jnp.minimum, jnp.where(vs[0] == m, ks[0], _BIGKEY))
        win = ks[0] == km
        vals.append(m[0:1])
        keys.append(km[0:1])
        for d in range(min(depth, k - r)):
            if d + 1 < depth:
                vs[d] = jnp.where(win, vs[d + 1], vs[d])
                ks[d] = jnp.where(win, ks[d + 1], ks[d])
            else:
                vs[d] = jnp.where(win, _NEG, vs[d])
    return jnp.concatenate(vals, axis=0), jnp.concatenate(keys, axis=0), tied


def _route_kernel(h_ref, g_ref, wq_ref, sk_ref, gm_ref, s_ref, val_ref, idx_ref, sel_ref, selt_ref,
                  gs_ref):
    step = pl.program_id(0)
    tr = h_ref.shape[0]
    nk = N_KEYS
    topk = PEER_TOPK
    nhp = 2 * PEER_HEADS

    @pl.when(step == 0)
    def _():
        selt_ref[...] = jnp.zeros_like(selt_ref)

    y = _rms(h_ref[...], g_ref[...]).astype(jnp.bfloat16)
    q = jnp.dot(y, wq_ref[...], preferred_element_type=jnp.float32).astype(jnp.bfloat16)
    for hp in range(nhp):
        s_ref[hp] = lax.dot_general(sk_ref[hp % 2], q[:, hp * nk:(hp + 1) * nk], _NT,
                                    preferred_element_type=jnp.float32)

    key_iota = lax.broadcasted_iota(jnp.int32, (nk, tr), 0).astype(jnp.float32)

    mrows = nk + 2 * G_ROW_PAD
    sub_l = lax.broadcasted_iota(jnp.int32, (mrows, LANES), 0).astype(jnp.float32)
    sub_r = lax.broadcasted_iota(jnp.int32, (nk, LANES), 0).astype(jnp.float32)
    zblk = jnp.zeros((nk, LANES), jnp.bfloat16)

    def token_pair(tp):
        lhs, rhs = [], []
        for s in range(2):
            t = 2 * tp + s
            gt = selt_ref[0, pl.ds(t, 1), :]
            it0 = selt_ref[1, pl.ds(t, 1), :] + float(s * G_ROW_PAD)
            it1 = selt_ref[2, pl.ds(t, 1), :]
            lhs.append(jnp.where(sub_l == it0, 1.0, 0.0).astype(jnp.bfloat16))
            rhs.append(jnp.where(sub_r == it1, gt, 0.0).astype(jnp.bfloat16))
        lhs2 = jnp.concatenate(lhs, axis=1)
        rhs2 = jnp.concatenate([jnp.concatenate([rhs[0], zblk], axis=1),
                                jnp.concatenate([zblk, rhs[1]], axis=1)], axis=0)
        g2 = lax.dot_general(lhs2, rhs2, _NT, preferred_element_type=jnp.float32)
        base = pl.multiple_of(2 * tp * G_ROW_STRIDE, SUBLANES)
        gs_ref[pl.ds(base, nk), :] = g2[:nk, :nk]
        gs_ref[pl.ds(base + nk, mrows), :] = g2[:, nk:]

    trips = nhp // TOPK_ILP
    pairs_per_trip = tr // 2 // trips

    def topk_and_onehot(j, tied):
        for s in range(TOPK_ILP):
            v, ix, td = _topk_distinct(s_ref[TOPK_ILP * j + s], key_iota, topk)
            val_ref[TOPK_ILP * j + s] = v
            idx_ref[TOPK_ILP * j + s] = ix
            tied = jnp.maximum(tied, td)
        for u in range(pairs_per_trip):
            token_pair(j * pairs_per_trip + u)
        return tied

    tied = lax.fori_loop(0, trips, topk_and_onehot, jnp.zeros((SUBLANES, tr), jnp.float32))

    @pl.when(jnp.max(tied) > 0.0)
    def _():
        def level1_ties(n, _):
            v, ix = _extract_topk(s_ref[n], key_iota, topk)
            val_ref[n] = v
            idx_ref[n] = ix
            return 0

        lax.fori_loop(0, nhp, level1_ties, 0)

    for i in range(nk):
        gm_ref[:, i * nk:(i + 1) * nk] = gs_ref[pl.ds(i, tr, stride=G_ROW_STRIDE), :].astype(gm_ref.dtype)

    half = topk // 2
    sub8 = lax.broadcasted_iota(jnp.int32, (half, tr), 0).astype(jnp.float32)
    sub16 = lax.broadcasted_iota(jnp.int32, (topk, tr), 0).astype(jnp.float32)
    code = lambda pos, a, b: pos * float(nk * nk) + a * float(nk) + b
    for hd in range(PEER_HEADS):
        v0, v1 = val_ref[2 * hd], val_ref[2 * hd + 1]
        i0, i1 = idx_ref[2 * hd], idx_ref[2 * hd + 1]
        cand = [v0[0:1] + v1]
        keys = [code(sub16, i0[0:1], i1)]
        for k0 in range(1, half):
            cand.append(v0[k0:k0 + 1] + v1[:half])
            keys.append(code(k0 * topk + sub8, i0[k0:k0 + 1], i1[:half]))
        cand.append(v0[half:] + v1[0:1])
        keys.append(code((half + sub8) * topk, i0[half:], i1[0:1]))
        tv, tk = _extract_topk(jnp.concatenate(cand, axis=0), jnp.concatenate(keys, axis=0), topk)
        e = jnp.exp(tv - tv[0:1])
        gate = e / jnp.sum(e, axis=0, keepdims=True)
        expert = tk.astype(jnp.int32) & (nk * nk - 1)
        rows = slice(hd * topk, (hd + 1) * topk)
        sel_ref[0, rows, :] = gate
        sel_ref[1, rows, :] = (expert >> 7).astype(jnp.float32)
        sel_ref[2, rows, :] = (expert & (nk - 1)).astype(jnp.float32)

    for a in range(3):
        selt_ref[a] = sel_ref[a].T


def _route(h, g, wq, sk):
    t, d = h.shape
    nq = wq.shape[1]
    nk = N_KEYS
    tr = LANES
    nblk = t // tr
    nhp = 2 * PEER_HEADS
    hk = PEER_HEADS * PEER_TOPK
    return pl.pallas_call(
        _route_kernel,
        grid=(nblk + 1,),
        in_specs=[pl.BlockSpec((tr, d), lambda i: (jnp.minimum(i, nblk - 1), 0)),
                  pl.BlockSpec((1, d), lambda i: (0, 0)),
                  pl.BlockSpec((d, nq), lambda i: (0, 0)),
                  pl.BlockSpec((2, nk, nk), lambda i: (0, 0, 0))],
        out_specs=pl.BlockSpec((tr, nk * nk), lambda i: (jnp.maximum(i - 1, 0), 0)),
        out_shape=jax.ShapeDtypeStruct((t, nk * nk), jnp.bfloat16),
        scratch_shapes=[pltpu.VMEM((nhp, nk, tr), jnp.float32),
                        pltpu.VMEM((nhp, PEER_TOPK, tr), jnp.float32),
                        pltpu.VMEM((nhp, PEER_TOPK, tr), jnp.float32),
                        pltpu.VMEM((3, hk, tr), jnp.float32),
                        pltpu.VMEM((3, tr, hk), jnp.float32),
                        pltpu.VMEM((tr * G_ROW_STRIDE, nk), jnp.float32)],
        compiler_params=_cparams("arbitrary"),
        name="peer_route",
    )(h, g, wq, sk)


def _peer_kernel(h_ref, g_ref, gm_ref, u_ref, v_ref, fg_ref, o_ref, y_ref, acc_ref, *, final_norm):
    e = pl.program_id(1)

    @pl.when(e == 0)
    def _():
        y_ref[...] = _rms(h_ref[...], g_ref[...]).astype(jnp.bfloat16)
        acc_ref[...] = jnp.zeros_like(acc_ref)

    hid = lax.dot_general(y_ref[...], u_ref[...], _NT, preferred_element_type=jnp.float32)
    act = 0.5 * hid * (1.0 + lax.erf(hid * (1.0 / math.sqrt(2.0))))
    a = (gm_ref[...].astype(jnp.float32) * act).astype(jnp.bfloat16)
    acc_ref[...] += jnp.dot(a, v_ref[...], preferred_element_type=jnp.float32)

    @pl.when(e == pl.num_programs(1) - 1)
    def _():
        out = h_ref[...] + acc_ref[...]
        if final_norm:
            out = _rms(out, fg_ref[...])
        o_ref[...] = out


def _peer(h, g, gm, u, v, layer, fg, tb, eb, final_norm):
    t, d = h.shape
    ne = v.shape[1]
    return pl.pallas_call(
        functools.partial(_peer_kernel, final_norm=final_norm),
        grid=(t // tb, ne // eb),
        in_specs=[pl.BlockSpec((tb, d), lambda i, e: (i, 0)),
                  pl.BlockSpec((1, d), lambda i, e: (0, 0)),
                  pl.BlockSpec((tb, eb), lambda i, e: (i, e)),
                  pl.BlockSpec((None, eb, d), lambda i, e: (layer, e, 0)),
                  pl.BlockSpec((None, eb, d), lambda i, e: (layer, e, 0)),
                  pl.BlockSpec((1, d), lambda i, e: (0, 0))],
        out_specs=pl.BlockSpec((tb, d), lambda i, e: (i, 0)),
        out_shape=jax.ShapeDtypeStruct((t, d), jnp.float32),
        scratch_shapes=[pltpu.VMEM((tb, d), jnp.bfloat16),
                        pltpu.VMEM((tb, d), jnp.float32)],
        compiler_params=_cparams("parallel", "arbitrary"),
        name="peer_experts",
    )(h, g, gm, u, v, fg)


def _seq_block(lp):
    return 3 * BLOCK_Q if lp % (3 * BLOCK_Q) == 0 else BLOCK_Q


def _forward(x, meta_tokens, mix_norm_g, ffn_norm_g, final_norm_g, ab_w_in, ab_conv_w,
             ab_w_out, cf_w_pw1, cf_b_pw1, cf_conv_w, cf_conv_b, cf_ln_g, cf_ln_b,
             cf_w_pw2, cf_b_pw2, peer_w_q, peer_sub_keys, peer_u, peer_v,
             peer_tb, peer_eb):
    bsz, s, d = x.shape
    l = s + N_META
    lp = -(-l // BLOCK_Q) * BLOCK_Q
    t = bsz * lp
    depth = mix_norm_g.shape[0]
    seq_blk = _seq_block(lp)
    bf = jnp.bfloat16
    row = lambda a: a.reshape(1, -1)

    h = jnp.concatenate([
        jnp.broadcast_to(meta_tokens.astype(x.dtype)[None], (bsz, N_META, d)),
        x,
        jnp.zeros((bsz, lp - l, d), x.dtype)], axis=1).reshape(t, d)

    a_width = A_HEADS * A_HEAD_DIM
    tri = (jnp.arange(BLOCK_Q)[:, None] > jnp.arange(BLOCK_Q)[None, :]).astype(bf)
    mhalf = jnp.concatenate([tri, jnp.ones((BLOCK_Q, BLOCK_Q), bf)], axis=1)
    mcat = jnp.concatenate([mhalf, mhalf], axis=0)
    qscale = jnp.concatenate([jnp.full((a_width,), 1.0 / math.sqrt(A_HEAD_DIM), jnp.float32),
                              jnp.ones((ab_w_in.shape[2] - a_width,), jnp.float32)])

    u_bf, v_bf = peer_u.astype(bf), peer_v.astype(bf)
    for i in range(depth):
        j = i // 2
        if i % 2 == 0:
            w_in = (ab_w_in[j] * qscale[None, :]).astype(bf)
            qkv, bb = _inproj(h, row(mix_norm_g[i]), w_in, 3 * a_width, seq_blk)
            a_out = _attention(qkv, mcat, bsz, lp, BLOCK_Q)
            h = _about(a_out, bb, ab_conv_w[j], ab_w_out[j].astype(bf), h, lp, seq_blk)
        else:
            z = _glu(h, row(mix_norm_g[i]), cf_w_pw1[j].astype(bf), row(cf_b_pw1[j]), seq_blk)
            h = _cfout(z, cf_conv_w[j], row(cf_conv_b[j]), row(cf_ln_g[j]), row(cf_ln_b[j]),
                       cf_w_pw2[j].astype(bf), row(cf_b_pw2[j]), h, lp, seq_blk)
        gm = _route(h, row(ffn_norm_g[i]), peer_w_q[i].astype(bf), peer_sub_keys[i].astype(bf))
        h = _peer(h, row(ffn_norm_g[i]), gm, u_bf, v_bf, i, row(final_norm_g),
                  peer_tb, peer_eb, final_norm=(i == depth - 1))
    return h.reshape(bsz, lp, d)[:, N_META:N_META + s, :]


def kernel(x, meta_tokens, mix_norm_g, ffn_norm_g, final_norm_g, ab_w_in, ab_conv_w, ab_w_out, cf_w_pw1, cf_b_pw1, cf_conv_w, cf_conv_b, cf_ln_g, cf_ln_b, cf_w_pw2, cf_b_pw2, peer_w_q, peer_sub_keys, peer_u, peer_v):
    return _forward(x, meta_tokens, mix_norm_g, ffn_norm_g, final_norm_g, ab_w_in, ab_conv_w,
                    ab_w_out, cf_w_pw1, cf_b_pw1, cf_conv_w, cf_conv_b, cf_ln_g, cf_ln_b,
                    cf_w_pw2, cf_b_pw2, peer_w_q, peer_sub_keys, peer_u, peer_v,
                    peer_tb=768, peer_eb=1024)
```

```python
import functools
import math

import jax
import jax.numpy as jnp
from jax import lax
from jax.experimental import pallas as pl
from jax.experimental.pallas import tpu as pltpu

N_META = 16
BLOCK_Q = 128
A_HEADS = 8
A_HEAD_DIM = 64
B_CONV = 3
CF_CONV = 31
PEER_HEADS = 8
N_KEYS = 128
PEER_TOPK = 16
EPS = 1e-6

LANES = 128
SUBLANES = 8
VMEM_LIMIT = 48 * 1024 * 1024

_NT = (((1,), (1,)), ((), ()))


def _cparams(*sem):
    return pltpu.CompilerParams(dimension_semantics=sem, vmem_limit_bytes=VMEM_LIMIT)


def _rms(x, g):
    return x * lax.rsqrt(jnp.mean(x * x, axis=-1, keepdims=True) + EPS) * g


def _inproj_kernel(h_ref, g_ref, w_ref, qkv_ref, bb_ref, *, a3):
    y = _rms(h_ref[...], g_ref[...]).astype(jnp.bfloat16)
    p = jnp.dot(y, w_ref[...], preferred_element_type=jnp.float32)
    qkv_ref[...] = p[:, :a3].astype(jnp.bfloat16)
    bb_ref[...] = p[:, a3:]


def _inproj(h, g, w, a3, tm):
    t, d = h.shape
    n = w.shape[1]
    return pl.pallas_call(
        functools.partial(_inproj_kernel, a3=a3),
        grid=(t // tm,),
        in_specs=[pl.BlockSpec((tm, d), lambda i: (i, 0)),
                  pl.BlockSpec((1, d), lambda i: (0, 0)),
                  pl.BlockSpec((d, n), lambda i: (0, 0))],
        out_specs=[pl.BlockSpec((tm, a3), lambda i: (i, 0)),
                   pl.BlockSpec((tm, n - a3), lambda i: (i, 0))],
        out_shape=[jax.ShapeDtypeStruct((t, a3), jnp.bfloat16),
                   jax.ShapeDtypeStruct((t, n - a3), jnp.float32)],
        compiler_params=_cparams("parallel"),
        name="ab_inproj",
    )(h, g, w)


def _glu_kernel(h_ref, g_ref, w_ref, b_ref, z_ref, *, inner):
    y = _rms(h_ref[...], g_ref[...]).astype(jnp.bfloat16)
    p = jnp.dot(y, w_ref[...], preferred_element_type=jnp.float32) + b_ref[...]
    z_ref[...] = p[:, :inner] * jax.nn.sigmoid(p[:, inner:])


def _glu(h, g, w, b, tm):
    t, d = h.shape
    n = w.shape[1]
    inner = n // 2
    return pl.pallas_call(
        functools.partial(_glu_kernel, inner=inner),
        grid=(t // tm,),
        in_specs=[pl.BlockSpec((tm, d), lambda i: (i, 0)),
                  pl.BlockSpec((1, d), lambda i: (0, 0)),
                  pl.BlockSpec((d, n), lambda i: (0, 0)),
                  pl.BlockSpec((1, n), lambda i: (0, 0))],
        out_specs=pl.BlockSpec((tm, inner), lambda i: (i, 0)),
        out_shape=jax.ShapeDtypeStruct((t, inner), jnp.float32),
        compiler_params=_cparams("parallel"),
        name="cf_glu",
    )(h, g, w, b)


F32_EXP_ZERO_BELOW = -104.0
_NEG_INF = float("-inf")


def _attn_kernel(q_ref, k_ref, v_ref, m_ref, o_ref, c_ref, acc_ref, *, bq, pairs, qb):
    g = pl.program_id(1)
    head0 = lax.broadcasted_iota(jnp.int32, (bq, LANES), 1) < A_HEAD_DIM
    row = lax.broadcasted_iota(jnp.int32, (bq, 2 * bq), 0)
    col = lax.broadcasted_iota(jnp.int32, (bq, 2 * bq), 1) & (bq - 1)
    causal = col < row
    mcat = m_ref[...]

    def split_heads(x):
        zero = jnp.zeros_like(x)
        return jnp.concatenate([jnp.where(head0, x, zero), jnp.where(head0, zero, x)], axis=0)

    def tile(step, first):
        chains = [(m, p) for m in range(qb) for p in range(pairs)]
        js = [g * qb + m - step for m in range(qb)]
        krows = [pl.ds(pl.multiple_of(jnp.maximum(j, 0) * bq, bq), bq) for j in js]
        lbs, lhs = [], []
        for m, p in chains:
            lanes = slice(p * LANES, (p + 1) * LANES)
            z = lax.dot_general(q_ref[m * bq:(m + 1) * bq, lanes], split_heads(k_ref[krows[m], lanes]), _NT,
                                preferred_element_type=jnp.float32)
            lb = jnp.minimum(z, 0.0) - jnp.log(1.0 + jnp.exp(-jnp.abs(z)))
            ln = lb - z
            if first:
                ln = jnp.where(causal, ln, 0.0)
            hi = ln.astype(jnp.bfloat16)
            lo = (ln - hi.astype(jnp.float32)).astype(jnp.bfloat16)
            lbs.append(lb)
            for h in range(2):
                lhs.append(jnp.concatenate([hi[:, h * bq:(h + 1) * bq], lo[:, h * bq:(h + 1) * bq]], axis=1))
        st = jnp.dot(jnp.concatenate(lhs, axis=0), mcat, preferred_element_type=jnp.float32)
        live = None
        for n, (m, p) in enumerate(chains):
            lanes = slice(p * LANES, (p + 1) * LANES)
            s0 = st[(2 * n) * bq:(2 * n + 1) * bq]
            s1 = st[(2 * n + 1) * bq:(2 * n + 2) * bq]
            suffix = jnp.concatenate([s0[:, :bq], s1[:, :bq]], axis=1)
            total = jnp.concatenate([s0[:, bq:], s1[:, bq:]], axis=1)
            if first:
                w = jnp.where(causal, jnp.exp(lbs[n] + suffix), 0.0)
                c_new = total
            else:
                c = c_ref[n]
                w = jnp.where(js[m] >= 0, jnp.exp(lbs[n] + suffix + c), 0.0)
                c_new = c + total
            pv = jnp.dot(w.astype(jnp.bfloat16), split_heads(v_ref[krows[m], lanes]),
                         preferred_element_type=jnp.float32)
            acc_ref[n] = pv if first else acc_ref[n] + pv
            c_ref[n] = c_new
            cm = jnp.where(js[m] >= 1, jnp.maximum(c_new[:, :bq], c_new[:, bq:]), _NEG_INF)
            live = cm if live is None else jnp.maximum(live, cm)
        return jnp.max(live)

    live0 = tile(0, True)

    def cond(st):
        return st[1] >= F32_EXP_ZERO_BELOW

    def body(st):
        step = st[0]
        return step + 1, tile(step, False)

    lax.while_loop(cond, body, (1, live0))
    for m in range(qb):
        o_ref[m * bq:(m + 1) * bq, :] = jnp.concatenate(
            [acc_ref[m * pairs + p] for p in range(pairs)], axis=1).astype(o_ref.dtype)


def _attention(qkv, mcat, batch, lp, bq):
    t = qkv.shape[0]
    aw = A_HEADS * A_HEAD_DIM
    pairs = aw // LANES
    nq = lp // bq
    qb = 3 if nq % 3 == 0 else 1
    ng = nq // qb
    return pl.pallas_call(
        functools.partial(_attn_kernel, bq=bq, pairs=pairs, qb=qb),
        grid=(batch, ng),
        in_specs=[pl.BlockSpec((qb * bq, aw), lambda b, i: (b * ng + i, 0)),
                  pl.BlockSpec((lp, aw), lambda b, i: (b, 1)),
                  pl.BlockSpec((lp, aw), lambda b, i: (b, 2)),
                  pl.BlockSpec((2 * bq, 2 * bq), lambda b, i: (0, 0))],
        out_specs=pl.BlockSpec((qb * bq, aw), lambda b, i: (b * ng + i, 0)),
        out_shape=jax.ShapeDtypeStruct((t, aw), jnp.bfloat16),
        scratch_shapes=[pltpu.VMEM((qb * pairs, bq, 2 * bq), jnp.float32),
                        pltpu.VMEM((qb * pairs, bq, LANES), jnp.float32)],
        compiler_params=_cparams("parallel", "arbitrary"),
        name="sb_attention",
    )(qkv, qkv, qkv, mcat)


def _about_kernel(a_ref, bb_ref, halo_ref, cw_ref, w_ref, h_ref, o_ref, *, nb, bw):
    i = pl.program_id(0)
    bb = bb_ref[...]
    gate_b, gate_c, hb = bb[:, :bw], bb[:, bw:2 * bw], bb[:, 2 * bw:]
    u = gate_c * hb
    hal = halo_ref[...]
    uh = hal[:, bw:2 * bw] * hal[:, 2 * bw:]
    uh = jnp.where(i % nb == 0, 0.0, uh)
    ucat = jnp.concatenate([uh, u], axis=0)
    tm = u.shape[0]
    cw = cw_ref[...]
    conv = u * cw[B_CONV - 1:B_CONV, :]
    for s in range(1, B_CONV):
        us = pltpu.roll(ucat, s, 0)[SUBLANES:SUBLANES + tm, :]
        conv = conv + us * cw[B_CONV - 1 - s:B_CONV - s, :]
    b_out = (gate_b * conv).astype(jnp.bfloat16)
    w = w_ref[...]
    aw = a_ref.shape[1]
    o_ref[...] = (h_ref[...]
                  + jnp.dot(a_ref[...], w[:aw, :], preferred_element_type=jnp.float32)
                  + jnp.dot(b_out, w[aw:, :], preferred_element_type=jnp.float32))


def _about(a_out, bb, conv_w, w_out, h, lp, tm):
    t, d = h.shape
    aw = a_out.shape[1]
    bw = conv_w.shape[1]
    nb = lp // tm
    hb = tm // SUBLANES
    return pl.pallas_call(
        functools.partial(_about_kernel, nb=nb, bw=bw),
        grid=(t // tm,),
        in_specs=[pl.BlockSpec((tm, aw), lambda i: (i, 0)),
                  pl.BlockSpec((tm, 3 * bw), lambda i: (i, 0)),
                  pl.BlockSpec((SUBLANES, 3 * bw), lambda i: (jnp.maximum(i * hb - 1, 0), 0)),
                  pl.BlockSpec((B_CONV, bw), lambda i: (0, 0)),
                  pl.BlockSpec((aw + bw, d), lambda i: (0, 0)),
                  pl.BlockSpec((tm, d), lambda i: (i, 0))],
        out_specs=pl.BlockSpec((tm, d), lambda i: (i, 0)),
        out_shape=jax.ShapeDtypeStruct((t, d), jnp.float32),
        compiler_params=_cparams("parallel"),
        name="ab_out",
    )(a_out, bb, bb, conv_w, w_out, h)


CF_HALO = 32


def _cfout_kernel(z_ref, halo_ref, cw_ref, cb_ref, lg_ref, lb_ref, w_ref, b_ref, h_ref, o_ref,
                  zs_ref, cv_ref, zp_ref, *, nb, rc):
    i = pl.program_id(0)
    tm, c = z_ref.shape
    zs_ref[0:CF_HALO, :] = jnp.where(i % nb == 0, 0.0, halo_ref[...])
    zs_ref[CF_HALO:, :] = z_ref[...]

    span = tm + CF_HALO - SUBLANES

    def conv_cols(j, _):
        cols = pl.ds(pl.multiple_of(j * LANES, LANES), LANES)
        cw = cw_ref[:, cols]
        cb = cb_ref[:, cols]
        for ph in range(1, SUBLANES):
            zp_ref[ph, 0:span, :] = zs_ref[ph:ph + span, cols]
        for r in range(0, tm, rc):
            acc = jnp.zeros((rc, LANES), jnp.float32) + cb
            for k in range(CF_CONV):
                off = r + CF_HALO - (CF_CONV - 1) + k
                ph = off % SUBLANES
                tap = zs_ref[off:off + rc, cols] if ph == 0 else zp_ref[ph, off - ph:off - ph + rc, :]
                acc = acc + tap * cw[k:k + 1, :]
            cv_ref[r:r + rc, cols] = acc
        return 0

    lax.fori_loop(0, c // LANES, conv_cols, 0)
    acc = cv_ref[...]
    mu = jnp.mean(acc, axis=-1, keepdims=True)
    xc = acc - mu
    var = jnp.mean(xc * xc, axis=-1, keepdims=True)
    y = xc * lax.rsqrt(var + EPS) * lg_ref[...] + lb_ref[...]
    y = y * jax.nn.sigmoid(y)
    o_ref[...] = (h_ref[...] + b_ref[...]
                  + jnp.dot(y.astype(jnp.bfloat16), w_ref[...], preferred_element_type=jnp.float32))


def _cfout(z, conv_w, conv_b, ln_g, ln_b, w2, b2, h, lp, tm):
    t, d = h.shape
    c = z.shape[1]
    nb = lp // tm
    hb = tm // CF_HALO
    vec = lambda n: pl.BlockSpec((1, n), lambda i: (0, 0))
    return pl.pallas_call(
        functools.partial(_cfout_kernel, nb=nb, rc=BLOCK_Q),
        grid=(t // tm,),
        in_specs=[pl.BlockSpec((tm, c), lambda i: (i, 0)),
                  pl.BlockSpec((CF_HALO, c), lambda i: (jnp.maximum(i * hb - 1, 0), 0)),
                  pl.BlockSpec((CF_CONV, c), lambda i: (0, 0)),
                  vec(c), vec(c), vec(c),
                  pl.BlockSpec((c, d), lambda i: (0, 0)),
                  vec(d),
                  pl.BlockSpec((tm, d), lambda i: (i, 0))],
        out_specs=pl.BlockSpec((tm, d), lambda i: (i, 0)),
        out_shape=jax.ShapeDtypeStruct((t, d), jnp.float32),
        scratch_shapes=[pltpu.VMEM((tm + CF_HALO, c), jnp.float32),
                        pltpu.VMEM((tm, c), jnp.float32),
                        pltpu.VMEM((SUBLANES, tm + CF_HALO, LANES), jnp.float32)],
        compiler_params=_cparams("parallel"),
        name="cf_out",
    )(z, z, conv_w, conv_b, ln_g, ln_b, w2, b2, h)


_NEG = float("-inf")
_BIGKEY = float(2 ** 30)
G_ROW_PAD = 4
G_ROW_STRIDE = N_KEYS + G_ROW_PAD
TOPK_ILP = 4


def _extract_topk(x, key, k):
    vals, keys = [], []
    for _ in range(k):
        m = jnp.max(x, axis=0, keepdims=True)
        km = jnp.min(jnp.where(x == m, key, _BIGKEY), axis=0, keepdims=True)
        x = jnp.where(key == km, _NEG, x)
        vals.append(m)
        keys.append(km)
    return jnp.concatenate(vals, axis=0), jnp.concatenate(keys, axis=0)


def _batcher_network(n):
    def merge(lo, hi, r):
        step = 2 * r
        if step < hi - lo:
            yield from merge(lo, hi, step)
            yield from merge(lo + r, hi, step)
            yield from ((i, i + r) for i in range(lo + r, hi - r, step))
        else:
            yield (lo, lo + r)

    def sort(lo, hi):
        if hi - lo >= 1:
            mid = lo + (hi - lo) // 2
            yield from sort(lo, mid)
            yield from sort(mid + 1, hi)
            yield from merge(lo, hi, 1)

    return tuple(sort(0, n - 1))


def _sublane_all(op, x):
    shift = SUBLANES // 2
    while shift:
        x = op(x, pltpu.roll(x, shift, 0))
        shift //= 2
    return x


def _topk_distinct(x, key, k):
    depth = x.shape[0] // SUBLANES
    vs = [x[d * SUBLANES:(d + 1) * SUBLANES] for d in range(depth)]
    ks = [key[d * SUBLANES:(d + 1) * SUBLANES] for d in range(depth)]
    for i, j in _batcher_network(depth):
        swap = vs[j] > vs[i]
        vs[i], vs[j] = jnp.maximum(vs[i], vs[j]), jnp.minimum(vs[i], vs[j])
        ks[i], ks[j] = jnp.where(swap, ks[j], ks[i]), jnp.where(swap, ks[i], ks[j])
    vals, keys = [], []
    tied = jnp.zeros(vs[0].shape, jnp.float32)
    prev = None
    for r in range(k + 1):
        m = _sublane_all(jnp.maximum, vs[0])
        if prev is not None:
            tied = jnp.maximum(tied, jnp.where(m >= prev, 1.0, 0.0))
        prev = m
        if r == k:
            break
        km = _sublane_all(jnp.minimum, jnp.where(vs[0] == m, ks[0], _BIGKEY))
        win = ks[0] == km
        vals.append(m[0:1])
        keys.append(km[0:1])
        for d in range(min(depth, k - r)):
            if d + 1 < depth:
                vs[d] = jnp.where(win, vs[d + 1], vs[d])
                ks[d] = jnp.where(win, ks[d + 1], ks[d])
            else:
                vs[d] = jnp.where(win, _NEG, vs[d])
    return jnp.concatenate(vals, axis=0), jnp.concatenate(keys, axis=0), tied


def _route_kernel(h_ref, g_ref, wq_ref, sk_ref, gm_ref, s_ref, val_ref, idx_ref, sel_ref, selt_ref,
                  gs_ref):
    step = pl.program_id(0)
    tr = h_ref.shape[0]
    nk = N_KEYS
    topk = PEER_TOPK
    nhp = 2 * PEER_HEADS

    @pl.when(step == 0)
    def _():
        selt_ref[...] = jnp.zeros_like(selt_ref)

    y = _rms(h_ref[...], g_ref[...]).astype(jnp.bfloat16)
    q = jnp.dot(y, wq_ref[...], preferred_element_type=jnp.float32).astype(jnp.bfloat16)
    for hp in range(nhp):
        s_ref[hp] = lax.dot_general(sk_ref[hp % 2], q[:, hp * nk:(hp + 1) * nk], _NT,
                                    preferred_element_type=jnp.float32)

    key_iota = lax.broadcasted_iota(jnp.int32, (nk, tr), 0).astype(jnp.float32)

    mrows = nk + 2 * G_ROW_PAD
    sub_l = lax.broadcasted_iota(jnp.int32, (mrows, LANES), 0).astype(jnp.float32)
    sub_r = lax.broadcasted_iota(jnp.int32, (nk, LANES), 0).astype(jnp.float32)
    zblk = jnp.zeros((nk, LANES), jnp.bfloat16)

    def token_pair(tp):
        lhs, rhs = [], []
        for s in range(2):
            t = 2 * tp + s
            gt = selt_ref[0, pl.ds(t, 1), :]
            it0 = selt_ref[1, pl.ds(t, 1), :] + float(s * G_ROW_PAD)
            it1 = selt_ref[2, pl.ds(t, 1), :]
            lhs.append(jnp.where(sub_l == it0, 1.0, 0.0).astype(jnp.bfloat16))
            rhs.append(jnp.where(sub_r == it1, gt, 0.0).astype(jnp.bfloat16))
        lhs2 = jnp.concatenate(lhs, axis=1)
        rhs2 = jnp.concatenate([jnp.concatenate([rhs[0], zblk], axis=1),
                                jnp.concatenate([zblk, rhs[1]], axis=1)], axis=0)
        g2 = lax.dot_general(lhs2, rhs2, _NT, preferred_element_type=jnp.float32)
        base = pl.multiple_of(2 * tp * G_ROW_STRIDE, SUBLANES)
        gs_ref[pl.ds(base, nk), :] = g2[:nk, :nk]
        gs_ref[pl.ds(base + nk, mrows), :] = g2[:, nk:]

    half = topk // 2
    sub8 = lax.broadcasted_iota(jnp.int32, (half, tr), 0).astype(jnp.float32)
    sub16 = lax.broadcasted_iota(jnp.int32, (topk, tr), 0).astype(jnp.float32)
    code = lambda pos, a, b: pos * float(nk * nk) + a * float(nk) + b

    def level2(hd, v0, v1, i0, i1):
        cand = [v0[0:1] + v1]
        keys = [code(sub16, i0[0:1], i1)]
        for k0 in range(1, half):
            cand.append(v0[k0:k0 + 1] + v1[:half])
            keys.append(code(k0 * topk + sub8, i0[k0:k0 + 1], i1[:half]))
        cand.append(v0[half:] + v1[0:1])
        keys.append(code((half + sub8) * topk, i0[half:], i1[0:1]))
        tv, tk = _extract_topk(jnp.concatenate(cand, axis=0), jnp.concatenate(keys, axis=0), topk)
        e = jnp.exp(tv - tv[0:1])
        gate = e / jnp.sum(e, axis=0, keepdims=True)
        expert = tk.astype(jnp.int32) & (nk * nk - 1)
        rows = pl.ds(pl.multiple_of(hd * topk, topk), topk)
        sel_ref[0, rows, :] = gate
        sel_ref[1, rows, :] = (expert >> 7).astype(jnp.float32)
        sel_ref[2, rows, :] = (expert & (nk - 1)).astype(jnp.float32)

    trips = nhp // TOPK_ILP
    pairs_per_trip = tr // 2 // trips

    def topk_and_onehot(j, tied):
        found = []
        for s in range(TOPK_ILP):
            v, ix, td = _topk_distinct(s_ref[TOPK_ILP * j + s], key_iota, topk)
            val_ref[TOPK_ILP * j + s] = v
            idx_ref[TOPK_ILP * j + s] = ix
            tied = jnp.maximum(tied, td)
            found.append((v, ix))
        for s in range(TOPK_ILP // 2):
            (v0, i0), (v1, i1) = found[2 * s], found[2 * s + 1]
            level2((TOPK_ILP // 2) * j + s, v0, v1, i0, i1)
        for u in range(pairs_per_trip):
            token_pair(j * pairs_per_trip + u)
        return tied

    tied = lax.fori_loop(0, trips, topk_and_onehot, jnp.zeros((SUBLANES, tr), jnp.float32))

    @pl.when(jnp.max(tied) > 0.0)
    def _():
        def level1_ties(n, _):
            v, ix = _extract_topk(s_ref[n], key_iota, topk)
            val_ref[n] = v
            idx_ref[n] = ix
            return 0

        lax.fori_loop(0, nhp, level1_ties, 0)

        def level2_ties(hd, _):
            level2(hd, val_ref[2 * hd], val_ref[2 * hd + 1], idx_ref[2 * hd], idx_ref[2 * hd + 1])
            return 0

        lax.fori_loop(0, PEER_HEADS, level2_ties, 0)

    for i in range(nk):
        gm_ref[:, i * nk:(i + 1) * nk] = gs_ref[pl.ds(i, tr, stride=G_ROW_STRIDE), :].astype(gm_ref.dtype)

    for a in range(3):
        selt_ref[a] = sel_ref[a].T


def _route(h, g, wq, sk):
    t, d = h.shape
    nq = wq.shape[1]
    nk = N_KEYS
    tr = LANES
    nblk = t // tr
    nhp = 2 * PEER_HEADS
    hk = PEER_HEADS * PEER_TOPK
    return pl.pallas_call(
        _route_kernel,
        grid=(nblk + 1,),
        in_specs=[pl.BlockSpec((tr, d), lambda i: (jnp.minimum(i, nblk - 1), 0)),
                  pl.BlockSpec((1, d), lambda i: (0, 0)),
                  pl.BlockSpec((d, nq), lambda i: (0, 0)),
                  pl.BlockSpec((2, nk, nk), lambda i: (0, 0, 0))],
        out_specs=pl.BlockSpec((tr, nk * nk), lambda i: (jnp.maximum(i - 1, 0), 0)),
        out_shape=jax.ShapeDtypeStruct((t, nk * nk), jnp.bfloat16),
        scratch_shapes=[pltpu.VMEM((nhp, nk, tr), jnp.float32),
                        pltpu.VMEM((nhp, PEER_TOPK, tr), jnp.float32),
                        pltpu.VMEM((nhp, PEER_TOPK, tr), jnp.float32),
                        pltpu.VMEM((3, hk, tr), jnp.float32),
                        pltpu.VMEM((3, tr, hk), jnp.float32),
                        pltpu.VMEM((tr * G_ROW_STRIDE, nk), jnp.float32)],
        compiler_params=_cparams("arbitrary"),
        name="peer_route",
    )(h, g, wq, sk)


def _peer_kernel(h_ref, g_ref, gm_ref, u_ref, v_ref, fg_ref, o_ref, y_ref, acc_ref, *, final_norm):
    e = pl.program_id(1)

    @pl.when(e == 0)
    def _():
        y_ref[...] = _rms(h_ref[...], g_ref[...]).astype(jnp.bfloat16)
        acc_ref[...] = jnp.zeros_like(acc_ref)

    hid = lax.dot_general(y_ref[...], u_ref[...], _NT, preferred_element_type=jnp.float32)
    act = 0.5 * hid * (1.0 + lax.erf(hid * (1.0 / math.sqrt(2.0))))
    a = (gm_ref[...].astype(jnp.float32) * act).astype(jnp.bfloat16)
    acc_ref[...] += jnp.dot(a, v_ref[...], preferred_element_type=jnp.float32)

    @pl.when(e == pl.num_programs(1) - 1)
    def _():
        out = h_ref[...] + acc_ref[...]
        if final_norm:
            out = _rms(out, fg_ref[...])
        o_ref[...] = out


def _peer(h, g, gm, u, v, layer, fg, tb, eb, final_norm):
    t, d = h.shape
    ne = v.shape[1]
    return pl.pallas_call(
        functools.partial(_peer_kernel, final_norm=final_norm),
        grid=(t // tb, ne // eb),
        in_specs=[pl.BlockSpec((tb, d), lambda i, e: (i, 0)),
                  pl.BlockSpec((1, d), lambda i, e: (0, 0)),
                  pl.BlockSpec((tb, eb), lambda i, e: (i, e)),
                  pl.BlockSpec((None, eb, d), lambda i, e: (layer, e, 0)),
                  pl.BlockSpec((None, eb, d), lambda i, e: (layer, e, 0)),
                  pl.BlockSpec((1, d), lambda i, e: (0, 0))],
        out_specs=pl.BlockSpec((tb, d), lambda i, e: (i, 0)),
        out_shape=jax.ShapeDtypeStruct((t, d), jnp.float32),
        scratch_shapes=[pltpu.VMEM((tb, d), jnp.bfloat16),
                        pltpu.VMEM((tb, d), jnp.float32)],
        compiler_params=_cparams("parallel", "arbitrary"),
        name="peer_experts",
    )(h, g, gm, u, v, fg)


def _seq_block(lp):
    return 3 * BLOCK_Q if lp % (3 * BLOCK_Q) == 0 else BLOCK_Q


def _forward(x, meta_tokens, mix_norm_g, ffn_norm_g, final_norm_g, ab_w_in, ab_conv_w,
             ab_w_out, cf_w_pw1, cf_b_pw1, cf_conv_w, cf_conv_b, cf_ln_g, cf_ln_b,
             cf_w_pw2, cf_b_pw2, peer_w_q, peer_sub_keys, peer_u, peer_v,
             peer_tb, peer_eb):
    bsz, s, d = x.shape
    l = s + N_META
    lp = -(-l // BLOCK_Q) * BLOCK_Q
    t = bsz * lp
    depth = mix_norm_g.shape[0]
    seq_blk = _seq_block(lp)
    bf = jnp.bfloat16
    row = lambda a: a.reshape(1, -1)

    h = jnp.concatenate([
        jnp.broadcast_to(meta_tokens.astype(x.dtype)[None], (bsz, N_META, d)),
        x,
        jnp.zeros((bsz, lp - l, d), x.dtype)], axis=1).reshape(t, d)

    a_width = A_HEADS * A_HEAD_DIM
    tri = (jnp.arange(BLOCK_Q)[:, None] > jnp.arange(BLOCK_Q)[None, :]).astype(bf)
    mhalf = jnp.concatenate([tri, jnp.ones((BLOCK_Q, BLOCK_Q), bf)], axis=1)
    mcat = jnp.concatenate([mhalf, mhalf], axis=0)
    qscale = jnp.concatenate([jnp.full((a_width,), 1.0 / math.sqrt(A_HEAD_DIM), jnp.float32),
                              jnp.ones((ab_w_in.shape[2] - a_width,), jnp.float32)])

    u_bf, v_bf = peer_u.astype(bf), peer_v.astype(bf)
    for i in range(depth):
        j = i // 2
        if i % 2 == 0:
            w_in = (ab_w_in[j] * qscale[None, :]).astype(bf)
            qkv, bb = _inproj(h, row(mix_norm_g[i]), w_in, 3 * a_width, seq_blk)
            a_out = _attention(qkv, mcat, bsz, lp, BLOCK_Q)
            h = _about(a_out, bb, ab_conv_w[j], ab_w_out[j].astype(bf), h, lp, seq_blk)
        else:
            z = _glu(h, row(mix_norm_g[i]), cf_w_pw1[j].astype(bf), row(cf_b_pw1[j]), seq_blk)
            h = _cfout(z, cf_conv_w[j], row(cf_conv_b[j]), row(cf_ln_g[j]), row(cf_ln_b[j]),
                       cf_w_pw2[j].astype(bf), row(cf_b_pw2[j]), h, lp, seq_blk)
        gm = _route(h, row(ffn_norm_g[i]), peer_w_q[i].astype(bf), peer_sub_keys[i].astype(bf))
        h = _peer(h, row(ffn_norm_g[i]), gm, u_bf, v_bf, i, row(final_norm_g),
                  peer_tb, peer_eb, final_norm=(i == depth - 1))
    return h.reshape(bsz, lp, d)[:, N_META:N_META + s, :]


def kernel(x, meta_tokens, mix_norm_g, ffn_norm_g, final_norm_g, ab_w_in, ab_conv_w, ab_w_out, cf_w_pw1, cf_b_pw1, cf_conv_w, cf_conv_b, cf_ln_g, cf_ln_b, cf_w_pw2, cf_b_pw2, peer_w_q, peer_sub_keys, peer_u, peer_v):
    return _forward(x, meta_tokens, mix_norm_g, ffn_norm_g, final_norm_g, ab_w_in, ab_conv_w,
                    ab_w_out, cf_w_pw1, cf_b_pw1, cf_conv_w, cf_conv_b, cf_ln_g, cf_ln_b,
                    cf_w_pw2, cf_b_pw2, peer_w_q, peer_sub_keys, peer_u, peer_v,
                    peer_tb=768, peer_eb=1024)
```

```python
import functools
import math

import jax
import jax.numpy as jnp
from jax import lax
from jax.experimental import pallas as pl
from jax.experimental.pallas import tpu as pltpu

N_META = 16
BLOCK_Q = 128
A_HEADS = 8
A_HEAD_DIM = 64
B_CONV = 3
CF_CONV = 31
PEER_HEADS = 8
N_KEYS = 128
PEER_TOPK = 16
EPS = 1e-6

LANES = 128
SUBLANES = 8
VMEM_LIMIT = 48 * 1024 * 1024

_NT = (((1,), (1,)), ((), ()))


def _cparams(*sem):
    return pltpu.CompilerParams(dimension_semantics=sem, vmem_limit_bytes=VMEM_LIMIT)


def _rms(x, g):
    return x * lax.rsqrt(jnp.mean(x * x, axis=-1, keepdims=True) + EPS) * g


def _inproj_kernel(h_ref, g_ref, w_ref, qkv_ref, bb_ref, *, a3):
    y = _rms(h_ref[...], g_ref[...]).astype(jnp.bfloat16)
    p = jnp.dot(y, w_ref[...], preferred_element_type=jnp.float32)
    qkv_ref[...] = p[:, :a3].astype(jnp.bfloat16)
    bb_ref[...] = p[:, a3:]


def _inproj(h, g, w, a3, tm):
    t, d = h.shape
    n = w.shape[1]
    return pl.pallas_call(
        functools.partial(_inproj_kernel, a3=a3),
        grid=(t // tm,),
        in_specs=[pl.BlockSpec((tm, d), lambda i: (i, 0)),
                  pl.BlockSpec((1, d), lambda i: (0, 0)),
                  pl.BlockSpec((d, n), lambda i: (0, 0))],
        out_specs=[pl.BlockSpec((tm, a3), lambda i: (i, 0)),
                   pl.BlockSpec((tm, n - a3), lambda i: (i, 0))],
        out_shape=[jax.ShapeDtypeStruct((t, a3), jnp.bfloat16),
                   jax.ShapeDtypeStruct((t, n - a3), jnp.float32)],
        compiler_params=_cparams("parallel"),
        name="ab_inproj",
    )(h, g, w)


def _glu_kernel(h_ref, g_ref, w_ref, b_ref, z_ref, *, inner):
    y = _rms(h_ref[...], g_ref[...]).astype(jnp.bfloat16)
    p = jnp.dot(y, w_ref[...], preferred_element_type=jnp.float32) + b_ref[...]
    z_ref[...] = p[:, :inner] * jax.nn.sigmoid(p[:, inner:])


def _glu(h, g, w, b, tm):
    t, d = h.shape
    n = w.shape[1]
    inner = n // 2
    return pl.pallas_call(
        functools.partial(_glu_kernel, inner=inner),
        grid=(t // tm,),
        in_specs=[pl.BlockSpec((tm, d), lambda i: (i, 0)),
                  pl.BlockSpec((1, d), lambda i: (0, 0)),
                  pl.BlockSpec((d, n), lambda i: (0, 0)),
                  pl.BlockSpec((1, n), lambda i: (0, 0))],
        out_specs=pl.BlockSpec((tm, inner), lambda i: (i, 0)),
        out_shape=jax.ShapeDtypeStruct((t, inner), jnp.float32),
        compiler_params=_cparams("parallel"),
        name="cf_glu",
    )(h, g, w, b)


F32_EXP_ZERO_BELOW = -104.0
_NEG_INF = float("-inf")


def _attn_kernel(q_ref, k_ref, v_ref, m_ref, o_ref, c_ref, acc_ref, *, bq, pairs, qb):
    g = pl.program_id(1)
    head0 = lax.broadcasted_iota(jnp.int32, (bq, LANES), 1) < A_HEAD_DIM
    row = lax.broadcasted_iota(jnp.int32, (bq, 2 * bq), 0)
    col = lax.broadcasted_iota(jnp.int32, (bq, 2 * bq), 1) & (bq - 1)
    causal = col < row
    mcat = m_ref[...]

    def split_heads(x):
        zero = jnp.zeros_like(x)
        return jnp.concatenate([jnp.where(head0, x, zero), jnp.where(head0, zero, x)], axis=0)

    def tile(step, first):
        chains = [(m, p) for m in range(qb) for p in range(pairs)]
        js = [g * qb + m - step for m in range(qb)]
        krows = [pl.ds(pl.multiple_of(jnp.maximum(j, 0) * bq, bq), bq) for j in js]
        lbs, lhs = [], []
        for m, p in chains:
            lanes = slice(p * LANES, (p + 1) * LANES)
            z = lax.dot_general(q_ref[m * bq:(m + 1) * bq, lanes], split_heads(k_ref[krows[m], lanes]), _NT,
                                preferred_element_type=jnp.float32)
            lb = jnp.minimum(z, 0.0) - jnp.log(1.0 + jnp.exp(-jnp.abs(z)))
            ln = lb - z
            if first:
                ln = jnp.where(causal, ln, 0.0)
            hi = ln.astype(jnp.bfloat16)
            lo = (ln - hi.astype(jnp.float32)).astype(jnp.bfloat16)
            lbs.append(lb)
            for h in range(2):
                lhs.append(jnp.concatenate([hi[:, h * bq:(h + 1) * bq], lo[:, h * bq:(h + 1) * bq]], axis=1))
        st = jnp.dot(jnp.concatenate(lhs, axis=0), mcat, preferred_element_type=jnp.float32)
        live = None
        for n, (m, p) in enumerate(chains):
            lanes = slice(p * LANES, (p + 1) * LANES)
            s0 = st[(2 * n) * bq:(2 * n + 1) * bq]
            s1 = st[(2 * n + 1) * bq:(2 * n + 2) * bq]
            suffix = jnp.concatenate([s0[:, :bq], s1[:, :bq]], axis=1)
            total = jnp.concatenate([s0[:, bq:], s1[:, bq:]], axis=1)
            if first:
                w = jnp.where(causal, jnp.exp(lbs[n] + suffix), 0.0)
                c_new = total
            else:
                c = c_ref[n]
                w = jnp.where(js[m] >= 0, jnp.exp(lbs[n] + suffix + c), 0.0)
                c_new = c + total
            pv = jnp.dot(w.astype(jnp.bfloat16), split_heads(v_ref[krows[m], lanes]),
                         preferred_element_type=jnp.float32)
            acc_ref[n] = pv if first else acc_ref[n] + pv
            c_ref[n] = c_new
            cm = jnp.where(js[m] >= 1, jnp.maximum(c_new[:, :bq], c_new[:, bq:]), _NEG_INF)
            live = cm if live is None else jnp.maximum(live, cm)
        return jnp.max(live)

    live0 = tile(0, True)

    def cond(st):
        return st[1] >= F32_EXP_ZERO_BELOW

    def body(st):
        step = st[0]
        return step + 1, tile(step, False)

    lax.while_loop(cond, body, (1, live0))
    for m in range(qb):
        o_ref[m * bq:(m + 1) * bq, :] = jnp.concatenate(
            [acc_ref[m * pairs + p] for p in range(pairs)], axis=1).astype(o_ref.dtype)


def _attention(qkv, mcat, batch, lp, bq):
    t = qkv.shape[0]
    aw = A_HEADS * A_HEAD_DIM
    pairs = aw // LANES
    nq = lp // bq
    qb = 3 if nq % 3 == 0 else 1
    ng = nq // qb
    return pl.pallas_call(
        functools.partial(_attn_kernel, bq=bq, pairs=pairs, qb=qb),
        grid=(batch, ng),
        in_specs=[pl.BlockSpec((qb * bq, aw), lambda b, i: (b * ng + i, 0)),
                  pl.BlockSpec((lp, aw), lambda b, i: (b, 1)),
                  pl.BlockSpec((lp, aw), lambda b, i: (b, 2)),
                  pl.BlockSpec((2 * bq, 2 * bq), lambda b, i: (0, 0))],
        out_specs=pl.BlockSpec((qb * bq, aw), lambda b, i: (b * ng + i, 0)),
        out_shape=jax.ShapeDtypeStruct((t, aw), jnp.bfloat16),
        scratch_shapes=[pltpu.VMEM((qb * pairs, bq, 2 * bq), jnp.float32),
                        pltpu.VMEM((qb * pairs, bq, LANES), jnp.float32)],
        compiler_params=_cparams("parallel", "arbitrary"),
        name="sb_attention",
    )(qkv, qkv, qkv, mcat)


def _about_kernel(a_ref, bb_ref, halo_ref, cw_ref, w_ref, h_ref, o_ref, *, nb, bw):
    i = pl.program_id(0)
    bb = bb_ref[...]
    gate_b, gate_c, hb = bb[:, :bw], bb[:, bw:2 * bw], bb[:, 2 * bw:]
    u = gate_c * hb
    hal = halo_ref[...]
    uh = hal[:, bw:2 * bw] * hal[:, 2 * bw:]
    uh = jnp.where(i % nb == 0, 0.0, uh)
    ucat = jnp.concatenate([uh, u], axis=0)
    tm = u.shape[0]
    cw = cw_ref[...]
    conv = u * cw[B_CONV - 1:B_CONV, :]
    for s in range(1, B_CONV):
        us = pltpu.roll(ucat, s, 0)[SUBLANES:SUBLANES + tm, :]
        conv = conv + us * cw[B_CONV - 1 - s:B_CONV - s, :]
    b_out = (gate_b * conv).astype(jnp.bfloat16)
    w = w_ref[...]
    aw = a_ref.shape[1]
    o_ref[...] = (h_ref[...]
                  + jnp.dot(a_ref[...], w[:aw, :], preferred_element_type=jnp.float32)
                  + jnp.dot(b_out, w[aw:, :], preferred_element_type=jnp.float32))


def _about(a_out, bb, conv_w, w_out, h, lp, tm):
    t, d = h.shape
    aw = a_out.shape[1]
    bw = conv_w.shape[1]
    nb = lp // tm
    hb = tm // SUBLANES
    return pl.pallas_call(
        functools.partial(_about_kernel, nb=nb, bw=bw),
        grid=(t // tm,),
        in_specs=[pl.BlockSpec((tm, aw), lambda i: (i, 0)),
                  pl.BlockSpec((tm, 3 * bw), lambda i: (i, 0)),
                  pl.BlockSpec((SUBLANES, 3 * bw), lambda i: (jnp.maximum(i * hb - 1, 0), 0)),
                  pl.BlockSpec((B_CONV, bw), lambda i: (0, 0)),
                  pl.BlockSpec((aw + bw, d), lambda i: (0, 0)),
                  pl.BlockSpec((tm, d), lambda i: (i, 0))],
        out_specs=pl.BlockSpec((tm, d), lambda i: (i, 0)),
        out_shape=jax.ShapeDtypeStruct((t, d), jnp.float32),
        compiler_params=_cparams("parallel"),
        name="ab_out",
    )(a_out, bb, bb, conv_w, w_out, h)


CF_HALO = 32


def _cfout_kernel(z_ref, halo_ref, cw_ref, cb_ref, lg_ref, lb_ref, w_ref, b_ref, h_ref, o_ref,
                  zs_ref, cv_ref, zp_ref, *, nb, rc):
    i = pl.program_id(0)
    tm, c = z_ref.shape
    zs_ref[0:CF_HALO, :] = jnp.where(i % nb == 0, 0.0, halo_ref[...])
    zs_ref[CF_HALO:, :] = z_ref[...]

    span = tm + CF_HALO - SUBLANES

    def conv_cols(j, _):
        cols = pl.ds(pl.multiple_of(j * LANES, LANES), LANES)
        cw = cw_ref[:, cols]
        cb = cb_ref[:, cols]
        for ph in range(1, SUBLANES):
            zp_ref[ph, 0:span, :] = zs_ref[ph:ph + span, cols]
        for r in range(0, tm, rc):
            acc = jnp.zeros((rc, LANES), jnp.float32) + cb
            for k in range(CF_CONV):
                off = r + CF_HALO - (CF_CONV - 1) + k
                ph = off % SUBLANES
                tap = zs_ref[off:off + rc, cols] if ph == 0 else zp_ref[ph, off - ph:off - ph + rc, :]
                acc = acc + tap * cw[k:k + 1, :]
            cv_ref[r:r + rc, cols] = acc
        return 0

    lax.fori_loop(0, c // LANES, conv_cols, 0)
    acc = cv_ref[...]
    mu = jnp.mean(acc, axis=-1, keepdims=True)
    xc = acc - mu
    var = jnp.mean(xc * xc, axis=-1, keepdims=True)
    y = xc * lax.rsqrt(var + EPS) * lg_ref[...] + lb_ref[...]
    y = y * jax.nn.sigmoid(y)
    o_ref[...] = (h_ref[...] + b_ref[...]
                  + jnp.dot(y.astype(jnp.bfloat16), w_ref[...], preferred_element_type=jnp.float32))


def _cfout(z, conv_w, conv_b, ln_g, ln_b, w2, b2, h, lp, tm):
    t, d = h.shape
    c = z.shape[1]
    nb = lp // tm
    hb = tm // CF_HALO
    vec = lambda n: pl.BlockSpec((1, n), lambda i: (0, 0))
    return pl.pallas_call(
        functools.partial(_cfout_kernel, nb=nb, rc=BLOCK_Q),
        grid=(t // tm,),
        in_specs=[pl.BlockSpec((tm, c), lambda i: (i, 0)),
                  pl.BlockSpec((CF_HALO, c), lambda i: (jnp.maximum(i * hb - 1, 0), 0)),
                  pl.BlockSpec((CF_CONV, c), lambda i: (0, 0)),
                  vec(c), vec(c), vec(c),
                  pl.BlockSpec((c, d), lambda i: (0, 0)),
                  vec(d),
                  pl.BlockSpec((tm, d), lambda i: (i, 0))],
        out_specs=pl.BlockSpec((tm, d), lambda i: (i, 0)),
        out_shape=jax.ShapeDtypeStruct((t, d), jnp.float32),
        scratch_shapes=[pltpu.VMEM((tm + CF_HALO, c), jnp.float32),
                        pltpu.VMEM((tm, c), jnp.float32),
                        pltpu.VMEM((SUBLANES, tm + CF_HALO, LANES), jnp.float32)],
        compiler_params=_cparams("parallel"),
        name="cf_out",
    )(z, z, conv_w, conv_b, ln_g, ln_b, w2, b2, h)


_NEG = float("-inf")
_BIGKEY = float(2 ** 30)
G_ROW_PAD = 4
G_ROW_STRIDE = N_KEYS + G_ROW_PAD
TOPK_ILP = 4


def _extract_topk(x, key, k):
    vals, keys = [], []
    for _ in range(k):
        m = jnp.max(x, axis=0, keepdims=True)
        km = jnp.min(jnp.where(x == m, key, _BIGKEY), axis=0, keepdims=True)
        x = jnp.where(key == km, _NEG, x)
        vals.append(m)
        keys.append(km)
    return jnp.concatenate(vals, axis=0), jnp.concatenate(keys, axis=0)


def _batcher_network(n):
    def merge(lo, hi, r):
        step = 2 * r
        if step < hi - lo:
            yield from merge(lo, hi, step)
            yield from merge(lo + r, hi, step)
            yield from ((i, i + r) for i in range(lo + r, hi - r, step))
        else:
            yield (lo, lo + r)

    def sort(lo, hi):
        if hi - lo >= 1:
            mid = lo + (hi - lo) // 2
            yield from sort(lo, mid)
            yield from sort(mid + 1, hi)
            yield from merge(lo, hi, 1)

    return tuple(sort(0, n - 1))


def _sublane_all(op, x):
    shift = SUBLANES // 2
    while shift:
        x = op(x, pltpu.roll(x, shift, 0))
        shift //= 2
    return x


def _topk_distinct(x, key, k):
    depth = x.shape[0] // SUBLANES
    vs = [x[d * SUBLANES:(d + 1) * SUBLANES] for d in range(depth)]
    ks = [key[d * SUBLANES:(d + 1) * SUBLANES] for d in range(depth)]
    for i, j in _batcher_network(depth):
        swap = vs[j] > vs[i]
        vs[i], vs[j] = jnp.maximum(vs[i], vs[j]), jnp.minimum(vs[i], vs[j])
        ks[i], ks[j] = jnp.where(swap, ks[j], ks[i]), jnp.where(swap, ks[i], ks[j])
    vals, keys = [], []
    tied = jnp.zeros(vs[0].shape, jnp.float32)
    prev = None
    for r in range(k + 1):
        m = _sublane_all(jnp.maximum, vs[0])
        if prev is not None:
            tied = jnp.maximum(tied, jnp.where(m >= prev, 1.0, 0.0))
        prev = m
        if r == k:
            break
        km = _sublane_all(jnp.minimum, jnp.where(vs[0] == m, ks[0], _BIGKEY))
        win = ks[0] == km
        vals.append(m[0:1])
        keys.append(km[0:1])
        for d in range(min(depth, k - r)):
            if d + 1 < depth:
                vs[d] = jnp.where(win, vs[d + 1], vs[d])
                ks[d] = jnp.where(win, ks[d + 1], ks[d])
            else:
                vs[d] = jnp.where(win, _NEG, vs[d])
    return jnp.concatenate(vals, axis=0), jnp.concatenate(keys, axis=0), tied


def _route_kernel(h_ref, g_ref, wq_ref, sk_ref, gm_ref, s_ref, val_ref, idx_ref, sel_ref, selt_ref,
                  gs_ref):
    step = pl.program_id(0)
    tr = h_ref.shape[0]
    nk = N_KEYS
    topk = PEER_TOPK
    nhp = 2 * PEER_HEADS

    @pl.when(step == 0)
    def _():
        selt_ref[...] = jnp.zeros_like(selt_ref)

    y = _rms(h_ref[...], g_ref[...]).astype(jnp.bfloat16)
    q = jnp.dot(y, wq_ref[...], preferred_element_type=jnp.float32).astype(jnp.bfloat16)
    for hp in range(nhp):
        s_ref[hp] = lax.dot_general(sk_ref[hp % 2], q[:, hp * nk:(hp + 1) * nk], _NT,
                                    preferred_element_type=jnp.float32)

    key_iota = lax.broadcasted_iota(jnp.int32, (nk, tr), 0).astype(jnp.float32)

    mrows = nk + 2 * G_ROW_PAD
    sub_l = lax.broadcasted_iota(jnp.int32, (mrows, LANES), 0).astype(jnp.float32)
    sub_r = lax.broadcasted_iota(jnp.int32, (nk, LANES), 0).astype(jnp.float32)
    zblk = jnp.zeros((nk, LANES), jnp.bfloat16)

    def token_pair(tp):
        lhs, rhs = [], []
        for s in range(2):
            t = 2 * tp + s
            gt = selt_ref[0, pl.ds(t, 1), :]
            it0 = selt_ref[1, pl.ds(t, 1), :] + float(s * G_ROW_PAD)
            it1 = selt_ref[2, pl.ds(t, 1), :]
            lhs.append(jnp.where(sub_l == it0, 1.0, 0.0).astype(jnp.bfloat16))
            rhs.append(jnp.where(sub_r == it1, gt, 0.0).astype(jnp.bfloat16))
        lhs2 = jnp.concatenate(lhs, axis=1)
        rhs2 = jnp.concatenate([jnp.concatenate([rhs[0], zblk], axis=1),
                                jnp.concatenate([zblk, rhs[1]], axis=1)], axis=0)
        g2 = lax.dot_general(lhs2, rhs2, _NT, preferred_element_type=jnp.float32)
        base = pl.multiple_of(2 * tp * G_ROW_STRIDE, SUBLANES)
        gs_ref[pl.ds(base, nk), :] = g2[:nk, :nk]
        gs_ref[pl.ds(base + nk, mrows), :] = g2[:, nk:]

    half = topk // 2
    sub8 = lax.broadcasted_iota(jnp.int32, (half, tr), 0).astype(jnp.float32)
    sub16 = lax.broadcasted_iota(jnp.int32, (topk, tr), 0).astype(jnp.float32)
    code = lambda pos, a, b: pos * float(nk * nk) + a * float(nk) + b

    def level2(hd, v0, v1, i0, i1):
        cand = [v0[0:1] + v1]
        keys = [code(sub16, i0[0:1], i1)]
        for k0 in range(1, half):
            cand.append(v0[k0:k0 + 1] + v1[:half])
            keys.append(code(k0 * topk + sub8, i0[k0:k0 + 1], i1[:half]))
        cand.append(v0[half:] + v1[0:1])
        keys.append(code((half + sub8) * topk, i0[half:], i1[0:1]))
        tv, tk = _extract_topk(jnp.concatenate(cand, axis=0), jnp.concatenate(keys, axis=0), topk)
        e = jnp.exp(tv - tv[0:1])
        gate = e / jnp.sum(e, axis=0, keepdims=True)
        expert = tk.astype(jnp.int32) & (nk * nk - 1)
        rows = pl.ds(pl.multiple_of(hd * topk, topk), topk)
        sel_ref[0, rows, :] = gate
        sel_ref[1, rows, :] = (expert >> 7).astype(jnp.float32)
        sel_ref[2, rows, :] = (expert & (nk - 1)).astype(jnp.float32)

    trips = nhp // TOPK_ILP
    pairs_per_trip = tr // 2 // trips

    def topk_and_onehot(j, tied):
        found = []
        for s in range(TOPK_ILP):
            v, ix, td = _topk_distinct(s_ref[TOPK_ILP * j + s], key_iota, topk)
            val_ref[TOPK_ILP * j + s] = v
            idx_ref[TOPK_ILP * j + s] = ix
            tied = jnp.maximum(tied, td)
            found.append((v, ix))
        for s in range(TOPK_ILP // 2):
            (v0, i0), (v1, i1) = found[2 * s], found[2 * s + 1]
            level2((TOPK_ILP // 2) * j + s, v0, v1, i0, i1)
        for u in range(pairs_per_trip):
            token_pair(j * pairs_per_trip + u)
        return tied

    tied = lax.fori_loop(0, trips, topk_and_onehot, jnp.zeros((SUBLANES, tr), jnp.float32))

    @pl.when(jnp.max(tied) > 0.0)
    def _():
        def level1_ties(n, _):
            v, ix = _extract_topk(s_ref[n], key_iota, topk)
            val_ref[n] = v
            idx_ref[n] = ix
            return 0

        lax.fori_loop(0, nhp, level1_ties, 0)

        def level2_ties(hd, _):
            level2(hd, val_ref[2 * hd], val_ref[2 * hd + 1], idx_ref[2 * hd], idx_ref[2 * hd + 1])
            return 0

        lax.fori_loop(0, PEER_HEADS, level2_ties, 0)

    for i in range(nk):
        gm_ref[:, i * nk:(i + 1) * nk] = gs_ref[pl.ds(i, tr, stride=G_ROW_STRIDE), :].astype(gm_ref.dtype)

    for a in range(3):
        selt_ref[a] = sel_ref[a].T


def _route(h, g, wq, sk):
    t, d = h.shape
    nq = wq.shape[1]
    nk = N_KEYS
    tr = LANES
    nblk = t // tr
    nhp = 2 * PEER_HEADS
    hk = PEER_HEADS * PEER_TOPK
    return pl.pallas_call(
        _route_kernel,
        grid=(nblk + 1,),
        in_specs=[pl.BlockSpec((tr, d), lambda i: (jnp.minimum(i, nblk - 1), 0)),
                  pl.BlockSpec((1, d), lambda i: (0, 0)),
                  pl.BlockSpec((d, nq), lambda i: (0, 0)),
                  pl.BlockSpec((2, nk, nk), lambda i: (0, 0, 0))],
        out_specs=pl.BlockSpec((tr, nk * nk), lambda i: (jnp.maximum(i - 1, 0), 0)),
        out_shape=jax.ShapeDtypeStruct((t, nk * nk), jnp.bfloat16),
        scratch_shapes=[pltpu.VMEM((nhp, nk, tr), jnp.float32),
                        pltpu.VMEM((nhp, PEER_TOPK, tr), jnp.float32),
                        pltpu.VMEM((nhp, PEER_TOPK, tr), jnp.float32),
                        pltpu.VMEM((3, hk, tr), jnp.float32),
                        pltpu.VMEM((3, tr, hk), jnp.float32),
                        pltpu.VMEM((tr * G_ROW_STRIDE, nk), jnp.float32)],
        compiler_params=_cparams("arbitrary"),
        name="peer_route",
    )(h, g, wq, sk)


def _peer_kernel(h_ref, g_ref, gm_ref, u_ref, v_ref, fg_ref, o_ref, y_ref, acc_ref, *, final_norm):
    e = pl.program_id(1)

    @pl.when(e == 0)
    def _():
        y_ref[...] = _rms(h_ref[...], g_ref[...]).astype(jnp.bfloat16)
        acc_ref[...] = jnp.zeros_like(acc_ref)

    hid = lax.dot_general(y_ref[...], u_ref[...].astype(jnp.bfloat16), _NT, preferred_element_type=jnp.float32)
    act = 0.5 * hid * (1.0 + lax.erf(hid * (1.0 / math.sqrt(2.0))))
    a = (gm_ref[...].astype(jnp.float32) * act).astype(jnp.bfloat16)
    acc_ref[...] += jnp.dot(a, v_ref[...].astype(jnp.bfloat16), preferred_element_type=jnp.float32)

    @pl.when(e == pl.num_programs(1) - 1)
    def _():
        out = h_ref[...] + acc_ref[...]
        if final_norm:
            out = _rms(out, fg_ref[...])
        o_ref[...] = out


def _peer(h, g, gm, u, v, layer, fg, tb, eb, final_norm):
    t, d = h.shape
    ne = v.shape[1]
    return pl.pallas_call(
        functools.partial(_peer_kernel, final_norm=final_norm),
        grid=(t // tb, ne // eb),
        in_specs=[pl.BlockSpec((tb, d), lambda i, e: (i, 0)),
                  pl.BlockSpec((1, d), lambda i, e: (0, 0)),
                  pl.BlockSpec((tb, eb), lambda i, e: (i, e)),
                  pl.BlockSpec((None, eb, d), lambda i, e: (layer, e, 0)),
                  pl.BlockSpec((None, eb, d), lambda i, e: (layer, e, 0)),
                  pl.BlockSpec((1, d), lambda i, e: (0, 0))],
        out_specs=pl.BlockSpec((tb, d), lambda i, e: (i, 0)),
        out_shape=jax.ShapeDtypeStruct((t, d), jnp.float32),
        scratch_shapes=[pltpu.VMEM((tb, d), jnp.bfloat16),
                        pltpu.VMEM((tb, d), jnp.float32)],
        compiler_params=_cparams("parallel", "arbitrary"),
        name="peer_experts",
    )(h, g, gm, u, v, fg)


def _seq_block(lp):
    return 3 * BLOCK_Q if lp % (3 * BLOCK_Q) == 0 else BLOCK_Q


def _forward(x, meta_tokens, mix_norm_g, ffn_norm_g, final_norm_g, ab_w_in, ab_conv_w,
             ab_w_out, cf_w_pw1, cf_b_pw1, cf_conv_w, cf_conv_b, cf_ln_g, cf_ln_b,
             cf_w_pw2, cf_b_pw2, peer_w_q, peer_sub_keys, peer_u, peer_v,
             peer_tb, peer_eb):
    bsz, s, d = x.shape
    l = s + N_META
    lp = -(-l // BLOCK_Q) * BLOCK_Q
    t = bsz * lp
    depth = mix_norm_g.shape[0]
    seq_blk = _seq_block(lp)
    bf = jnp.bfloat16
    row = lambda a: a.reshape(1, -1)

    h = jnp.concatenate([
        jnp.broadcast_to(meta_tokens.astype(x.dtype)[None], (bsz, N_META, d)),
        x,
        jnp.zeros((bsz, lp - l, d), x.dtype)], axis=1).reshape(t, d)

    a_width = A_HEADS * A_HEAD_DIM
    tri = (jnp.arange(BLOCK_Q)[:, None] > jnp.arange(BLOCK_Q)[None, :]).astype(bf)
    mhalf = jnp.concatenate([tri, jnp.ones((BLOCK_Q, BLOCK_Q), bf)], axis=1)
    mcat = jnp.concatenate([mhalf, mhalf], axis=0)
    qscale = jnp.concatenate([jnp.full((a_width,), 1.0 / math.sqrt(A_HEAD_DIM), jnp.float32),
                              jnp.ones((ab_w_in.shape[2] - a_width,), jnp.float32)])

    for i in range(depth):
        j = i // 2
        if i % 2 == 0:
            w_in = (ab_w_in[j] * qscale[None, :]).astype(bf)
            qkv, bb = _inproj(h, row(mix_norm_g[i]), w_in, 3 * a_width, seq_blk)
            a_out = _attention(qkv, mcat, bsz, lp, BLOCK_Q)
            h = _about(a_out, bb, ab_conv_w[j], ab_w_out[j].astype(bf), h, lp, seq_blk)
        else:
            z = _glu(h, row(mix_norm_g[i]), cf_w_pw1[j].astype(bf), row(cf_b_pw1[j]), seq_blk)
            h = _cfout(z, cf_conv_w[j], row(cf_conv_b[j]), row(cf_ln_g[j]), row(cf_ln_b[j]),
                       cf_w_pw2[j].astype(bf), row(cf_b_pw2[j]), h, lp, seq_blk)
        gm = _route(h, row(ffn_norm_g[i]), peer_w_q[i].astype(bf), peer_sub_keys[i].astype(bf))
        h = _peer(h, row(ffn_norm_g[i]), gm, peer_u, peer_v, i, row(final_norm_g),
                  peer_tb, peer_eb, final_norm=(i == depth - 1))
    return h.reshape(bsz, lp, d)[:, N_META:N_META + s, :]


def kernel(x, meta_tokens, mix_norm_g, ffn_norm_g, final_norm_g, ab_w_in, ab_conv_w, ab_w_out, cf_w_pw1, cf_b_pw1, cf_conv_w, cf_conv_b, cf_ln_g, cf_ln_b, cf_w_pw2, cf_b_pw2, peer_w_q, peer_sub_keys, peer_u, peer_v):
    return _forward(x, meta_tokens, mix_norm_g, ffn_norm_g, final_norm_g, ab_w_in, ab_conv_w,
                    ab_w_out, cf_w_pw1, cf_b_pw1, cf_conv_w, cf_conv_b, cf_ln_g, cf_ln_b,
                    cf_w_pw2, cf_b_pw2, peer_w_q, peer_sub_keys, peer_u, peer_v,
                    peer_tb=768, peer_eb=1024)
```

```python
import functools
import math

import jax
import jax.numpy as jnp
from jax import lax
from jax.experimental import pallas as pl
from jax.experimental.pallas import tpu as pltpu

N_META = 16
BLOCK_Q = 128
A_HEADS = 8
A_HEAD_DIM = 64
B_CONV = 3
CF_CONV = 31
PEER_HEADS = 8
N_KEYS = 128
PEER_TOPK = 16
EPS = 1e-6

LANES = 128
SUBLANES = 8
VMEM_LIMIT = 48 * 1024 * 1024

_NT = (((1,), (1,)), ((), ()))


def _cparams(*sem):
    return pltpu.CompilerParams(dimension_semantics=sem, vmem_limit_bytes=VMEM_LIMIT)


def _rms(x, g):
    return x * lax.rsqrt(jnp.mean(x * x, axis=-1, keepdims=True) + EPS) * g


def _inproj_kernel(h_ref, g_ref, w_ref, qkv_ref, bb_ref, *, a3):
    y = _rms(h_ref[...], g_ref[...]).astype(jnp.bfloat16)
    p = jnp.dot(y, w_ref[...], preferred_element_type=jnp.float32)
    qkv_ref[...] = p[:, :a3].astype(jnp.bfloat16)
    bb_ref[...] = p[:, a3:]


def _inproj(h, g, w, a3, tm):
    t, d = h.shape
    n = w.shape[1]
    return pl.pallas_call(
        functools.partial(_inproj_kernel, a3=a3),
        grid=(t // tm,),
        in_specs=[pl.BlockSpec((tm, d), lambda i: (i, 0)),
                  pl.BlockSpec((1, d), lambda i: (0, 0)),
                  pl.BlockSpec((d, n), lambda i: (0, 0))],
        out_specs=[pl.BlockSpec((tm, a3), lambda i: (i, 0)),
                   pl.BlockSpec((tm, n - a3), lambda i: (i, 0))],
        out_shape=[jax.ShapeDtypeStruct((t, a3), jnp.bfloat16),
                   jax.ShapeDtypeStruct((t, n - a3), jnp.float32)],
        compiler_params=_cparams("parallel"),
        name="ab_inproj",
    )(h, g, w)


def _glu_kernel(h_ref, g_ref, w_ref, b_ref, z_ref, *, inner):
    y = _rms(h_ref[...], g_ref[...]).astype(jnp.bfloat16)
    p = jnp.dot(y, w_ref[...], preferred_element_type=jnp.float32) + b_ref[...]
    z_ref[...] = p[:, :inner] * jax.nn.sigmoid(p[:, inner:])


def _glu(h, g, w, b, tm):
    t, d = h.shape
    n = w.shape[1]
    inner = n // 2
    return pl.pallas_call(
        functools.partial(_glu_kernel, inner=inner),
        grid=(t // tm,),
        in_specs=[pl.BlockSpec((tm, d), lambda i: (i, 0)),
                  pl.BlockSpec((1, d), lambda i: (0, 0)),
                  pl.BlockSpec((d, n), lambda i: (0, 0)),
                  pl.BlockSpec((1, n), lambda i: (0, 0))],
        out_specs=pl.BlockSpec((tm, inner), lambda i: (i, 0)),
        out_shape=jax.ShapeDtypeStruct((t, inner), jnp.float32),
        compiler_params=_cparams("parallel"),
        name="cf_glu",
    )(h, g, w, b)


F32_EXP_ZERO_BELOW = -104.0
_NEG_INF = float("-inf")


def _attn_kernel(q_ref, k_ref, v_ref, m_ref, o_ref, c_ref, acc_ref, *, bq, pairs, qb):
    g = pl.program_id(1)
    head0 = lax.broadcasted_iota(jnp.int32, (bq, LANES), 1) < A_HEAD_DIM
    row = lax.broadcasted_iota(jnp.int32, (bq, 2 * bq), 0)
    col = lax.broadcasted_iota(jnp.int32, (bq, 2 * bq), 1) & (bq - 1)
    causal = col < row
    mcat = m_ref[...]

    def split_heads(x):
        zero = jnp.zeros_like(x)
        return jnp.concatenate([jnp.where(head0, x, zero), jnp.where(head0, zero, x)], axis=0)

    def tile(step, first):
        chains = [(m, p) for m in range(qb) for p in range(pairs)]
        js = [g * qb + m - step for m in range(qb)]
        krows = [pl.ds(pl.multiple_of(jnp.maximum(j, 0) * bq, bq), bq) for j in js]
        lbs, lhs = [], []
        for m, p in chains:
            lanes = slice(p * LANES, (p + 1) * LANES)
            z = lax.dot_general(q_ref[m * bq:(m + 1) * bq, lanes], split_heads(k_ref[krows[m], lanes]), _NT,
                                preferred_element_type=jnp.float32)
            lb = jnp.minimum(z, 0.0) - jnp.log(1.0 + jnp.exp(-jnp.abs(z)))
            ln = lb - z
            if first:
                ln = jnp.where(causal, ln, 0.0)
            hi = ln.astype(jnp.bfloat16)
            lo = (ln - hi.astype(jnp.float32)).astype(jnp.bfloat16)
            lbs.append(lb)
            for h in range(2):
                lhs.append(jnp.concatenate([hi[:, h * bq:(h + 1) * bq], lo[:, h * bq:(h + 1) * bq]], axis=1))
        st = jnp.dot(jnp.concatenate(lhs, axis=0), mcat, preferred_element_type=jnp.float32)
        live = None
        for n, (m, p) in enumerate(chains):
            lanes = slice(p * LANES, (p + 1) * LANES)
            s0 = st[(2 * n) * bq:(2 * n + 1) * bq]
            s1 = st[(2 * n + 1) * bq:(2 * n + 2) * bq]
            suffix = jnp.concatenate([s0[:, :bq], s1[:, :bq]], axis=1)
            total = jnp.concatenate([s0[:, bq:], s1[:, bq:]], axis=1)
            if first:
                w = jnp.where(causal, jnp.exp(lbs[n] + suffix), 0.0)
                c_new = total
            else:
                c = c_ref[n]
                w = jnp.where(js[m] >= 0, jnp.exp(lbs[n] + suffix + c), 0.0)
                c_new = c + total
            pv = jnp.dot(w.astype(jnp.bfloat16), split_heads(v_ref[krows[m], lanes]),
                         preferred_element_type=jnp.float32)
            acc_ref[n] = pv if first else acc_ref[n] + pv
            c_ref[n] = c_new
            cm = jnp.where(js[m] >= 1, jnp.maximum(c_new[:, :bq], c_new[:, bq:]), _NEG_INF)
            live = cm if live is None else jnp.maximum(live, cm)
        return jnp.max(live)

    live0 = tile(0, True)

    def cond(st):
        return st[1] >= F32_EXP_ZERO_BELOW

    def body(st):
        step = st[0]
        return step + 1, tile(step, False)

    lax.while_loop(cond, body, (1, live0))
    for m in range(qb):
        o_ref[m * bq:(m + 1) * bq, :] = jnp.concatenate(
            [acc_ref[m * pairs + p] for p in range(pairs)], axis=1).astype(o_ref.dtype)


def _attention(qkv, mcat, batch, lp, bq):
    t = qkv.shape[0]
    aw = A_HEADS * A_HEAD_DIM
    pairs = aw // LANES
    nq = lp // bq
    qb = 3 if nq % 3 == 0 else 1
    ng = nq // qb
    return pl.pallas_call(
        functools.partial(_attn_kernel, bq=bq, pairs=pairs, qb=qb),
        grid=(batch, ng),
        in_specs=[pl.BlockSpec((qb * bq, aw), lambda b, i: (b * ng + i, 0)),
                  pl.BlockSpec((lp, aw), lambda b, i: (b, 1)),
                  pl.BlockSpec((lp, aw), lambda b, i: (b, 2)),
                  pl.BlockSpec((2 * bq, 2 * bq), lambda b, i: (0, 0))],
        out_specs=pl.BlockSpec((qb * bq, aw), lambda b, i: (b * ng + i, 0)),
        out_shape=jax.ShapeDtypeStruct((t, aw), jnp.bfloat16),
        scratch_shapes=[pltpu.VMEM((qb * pairs, bq, 2 * bq), jnp.float32),
                        pltpu.VMEM((qb * pairs, bq, LANES), jnp.float32)],
        compiler_params=_cparams("parallel", "arbitrary"),
        name="sb_attention",
    )(qkv, qkv, qkv, mcat)


def _about_kernel(a_ref, bb_ref, halo_ref, cw_ref, w_ref, h_ref, o_ref, *, nb, bw):
    i = pl.program_id(0)
    bb = bb_ref[...]
    gate_b, gate_c, hb = bb[:, :bw], bb[:, bw:2 * bw], bb[:, 2 * bw:]
    u = gate_c * hb
    hal = halo_ref[...]
    uh = hal[:, bw:2 * bw] * hal[:, 2 * bw:]
    uh = jnp.where(i % nb == 0, 0.0, uh)
    ucat = jnp.concatenate([uh, u], axis=0)
    tm = u.shape[0]
    cw = cw_ref[...]
    conv = u * cw[B_CONV - 1:B_CONV, :]
    for s in range(1, B_CONV):
        us = pltpu.roll(ucat, s, 0)[SUBLANES:SUBLANES + tm, :]
        conv = conv + us * cw[B_CONV - 1 - s:B_CONV - s, :]
    b_out = (gate_b * conv).astype(jnp.bfloat16)
    w = w_ref[...]
    aw = a_ref.shape[1]
    o_ref[...] = (h_ref[...]
                  + jnp.dot(a_ref[...], w[:aw, :], preferred_element_type=jnp.float32)
                  + jnp.dot(b_out, w[aw:, :], preferred_element_type=jnp.float32))


def _about(a_out, bb, conv_w, w_out, h, lp, tm):
    t, d = h.shape
    aw = a_out.shape[1]
    bw = conv_w.shape[1]
    nb = lp // tm
    hb = tm // SUBLANES
    return pl.pallas_call(
        functools.partial(_about_kernel, nb=nb, bw=bw),
        grid=(t // tm,),
        in_specs=[pl.BlockSpec((tm, aw), lambda i: (i, 0)),
                  pl.BlockSpec((tm, 3 * bw), lambda i: (i, 0)),
                  pl.BlockSpec((SUBLANES, 3 * bw), lambda i: (jnp.maximum(i * hb - 1, 0), 0)),
                  pl.BlockSpec((B_CONV, bw), lambda i: (0, 0)),
                  pl.BlockSpec((aw + bw, d), lambda i: (0, 0)),
                  pl.BlockSpec((tm, d), lambda i: (i, 0))],
        out_specs=pl.BlockSpec((tm, d), lambda i: (i, 0)),
        out_shape=jax.ShapeDtypeStruct((t, d), jnp.float32),
        compiler_params=_cparams("parallel"),
        name="ab_out",
    )(a_out, bb, bb, conv_w, w_out, h)


CF_HALO = 32


def _cfout_kernel(z_ref, halo_ref, cw_ref, cb_ref, lg_ref, lb_ref, w_ref, b_ref, h_ref, o_ref,
                  zs_ref, cv_ref, zp_ref, *, nb, rc):
    i = pl.program_id(0)
    tm, c = z_ref.shape
    zs_ref[0:CF_HALO, :] = jnp.where(i % nb == 0, 0.0, halo_ref[...])
    zs_ref[CF_HALO:, :] = z_ref[...]

    span = tm + CF_HALO - SUBLANES

    def conv_cols(j, _):
        cols = pl.ds(pl.multiple_of(j * LANES, LANES), LANES)
        cw = cw_ref[:, cols]
        cb = cb_ref[:, cols]
        for ph in range(1, SUBLANES):
            zp_ref[ph, 0:span, :] = zs_ref[ph:ph + span, cols]
        for r in range(0, tm, rc):
            acc = jnp.zeros((rc, LANES), jnp.float32) + cb
            for k in range(CF_CONV):
                off = r + CF_HALO - (CF_CONV - 1) + k
                ph = off % SUBLANES
                tap = zs_ref[off:off + rc, cols] if ph == 0 else zp_ref[ph, off - ph:off - ph + rc, :]
                acc = acc + tap * cw[k:k + 1, :]
            cv_ref[r:r + rc, cols] = acc
        return 0

    lax.fori_loop(0, c // LANES, conv_cols, 0)
    acc = cv_ref[...]
    mu = jnp.mean(acc, axis=-1, keepdims=True)
    xc = acc - mu
    var = jnp.mean(xc * xc, axis=-1, keepdims=True)
    y = xc * lax.rsqrt(var + EPS) * lg_ref[...] + lb_ref[...]
    y = y * jax.nn.sigmoid(y)
    o_ref[...] = (h_ref[...] + b_ref[...]
                  + jnp.dot(y.astype(jnp.bfloat16), w_ref[...], preferred_element_type=jnp.float32))


def _cfout(z, conv_w, conv_b, ln_g, ln_b, w2, b2, h, lp, tm):
    t, d = h.shape
    c = z.shape[1]
    nb = lp // tm
    hb = tm // CF_HALO
    vec = lambda n: pl.BlockSpec((1, n), lambda i: (0, 0))
    return pl.pallas_call(
        functools.partial(_cfout_kernel, nb=nb, rc=BLOCK_Q),
        grid=(t // tm,),
        in_specs=[pl.BlockSpec((tm, c), lambda i: (i, 0)),
                  pl.BlockSpec((CF_HALO, c), lambda i: (jnp.maximum(i * hb - 1, 0), 0)),
                  pl.BlockSpec((CF_CONV, c), lambda i: (0, 0)),
                  vec(c), vec(c), vec(c),
                  pl.BlockSpec((c, d), lambda i: (0, 0)),
                  vec(d),
                  pl.BlockSpec((tm, d), lambda i: (i, 0))],
        out_specs=pl.BlockSpec((tm, d), lambda i: (i, 0)),
        out_shape=jax.ShapeDtypeStruct((t, d), jnp.float32),
        scratch_shapes=[pltpu.VMEM((tm + CF_HALO, c), jnp.float32),
                        pltpu.VMEM((tm, c), jnp.float32),
                        pltpu.VMEM((SUBLANES, tm + CF_HALO, LANES), jnp.float32)],
        compiler_params=_cparams("parallel"),
        name="cf_out",
    )(z, z, conv_w, conv_b, ln_g, ln_b, w2, b2, h)


_NEG = float("-inf")
_BIGKEY = float(2 ** 30)
KEY_BITS = N_KEYS.bit_length() - 1
G_ROW_PAD = 4
G_ROW_STRIDE = N_KEYS + G_ROW_PAD
TOPK_ILP = 4


def _extract_topk(x, key, k):
    vals, keys = [], []
    for _ in range(k):
        m = jnp.max(x, axis=0, keepdims=True)
        km = jnp.min(jnp.where(x == m, key, _BIGKEY), axis=0, keepdims=True)
        x = jnp.where(key == km, _NEG, x)
        vals.append(m)
        keys.append(km)
    return jnp.concatenate(vals, axis=0), jnp.concatenate(keys, axis=0)


def _batcher_network(n):
    def merge(lo, hi, r):
        step = 2 * r
        if step < hi - lo:
            yield from merge(lo, hi, step)
            yield from merge(lo + r, hi, step)
            yield from ((i, i + r) for i in range(lo + r, hi - r, step))
        else:
            yield (lo, lo + r)

    def sort(lo, hi):
        if hi - lo >= 1:
            mid = lo + (hi - lo) // 2
            yield from sort(lo, mid)
            yield from sort(mid + 1, hi)
            yield from merge(lo, hi, 1)

    return tuple(sort(0, n - 1))


def _sublane_all(op, x):
    shift = SUBLANES // 2
    while shift:
        x = op(x, pltpu.roll(x, shift, 0))
        shift //= 2
    return x


def _topk_distinct(x, key, k):
    depth = x.shape[0] // SUBLANES
    vs = [x[d * SUBLANES:(d + 1) * SUBLANES] for d in range(depth)]
    ks = [key[d * SUBLANES:(d + 1) * SUBLANES] for d in range(depth)]
    for i, j in _batcher_network(depth):
        swap = vs[j] > vs[i]
        vs[i], vs[j] = jnp.maximum(vs[i], vs[j]), jnp.minimum(vs[i], vs[j])
        ks[i], ks[j] = jnp.where(swap, ks[j], ks[i]), jnp.where(swap, ks[i], ks[j])
    vals, keys = [], []
    tied = jnp.zeros(vs[0].shape, jnp.float32)
    prev = None
    for r in range(k + 1):
        m = _sublane_all(jnp.maximum, vs[0])
        if prev is not None:
            tied = jnp.maximum(tied, jnp.where(m >= prev, 1.0, 0.0))
        prev = m
        if r == k:
            break
        km = _sublane_all(jnp.minimum, jnp.where(vs[0] == m, ks[0], _BIGKEY))
        win = ks[0] == km
        vals.append(m[0:1])
        keys.append(km[0:1])
        for d in range(min(depth, k - r)):
            if d + 1 < depth:
                vs[d] = jnp.where(win, vs[d + 1], vs[d])
                ks[d] = jnp.where(win, ks[d + 1], ks[d])
            else:
                vs[d] = jnp.where(win, _NEG, vs[d])
    return jnp.concatenate(vals, axis=0), jnp.concatenate(keys, axis=0), tied


def _route_kernel(h_ref, g_ref, wq_ref, sk_ref, gm_ref, s_ref, val_ref, idx_ref, sel_ref, selt_ref,
                  gs_ref):
    step = pl.program_id(0)
    tr = h_ref.shape[0]
    nk = N_KEYS
    topk = PEER_TOPK
    nhp = 2 * PEER_HEADS

    @pl.when(step == 0)
    def _():
        selt_ref[...] = jnp.zeros_like(selt_ref)

    y = _rms(h_ref[...], g_ref[...]).astype(jnp.bfloat16)
    q = jnp.dot(y, wq_ref[...], preferred_element_type=jnp.float32).astype(jnp.bfloat16)
    for hp in range(nhp):
        s_ref[hp] = lax.dot_general(sk_ref[hp % 2], q[:, hp * nk:(hp + 1) * nk], _NT,
                                    preferred_element_type=jnp.float32)

    key_iota = lax.broadcasted_iota(jnp.int32, (nk, tr), 0).astype(jnp.float32)

    mrows = nk + 2 * G_ROW_PAD
    sub_l = lax.broadcasted_iota(jnp.int32, (mrows, LANES), 0).astype(jnp.float32)
    sub_r = lax.broadcasted_iota(jnp.int32, (nk, LANES), 0).astype(jnp.float32)
    zblk = jnp.zeros((nk, LANES), jnp.bfloat16)

    def token_pair(tp):
        lhs, rhs = [], []
        for s in range(2):
            t = 2 * tp + s
            gt = selt_ref[0, pl.ds(t, 1), :]
            it0 = selt_ref[1, pl.ds(t, 1), :] + float(s * G_ROW_PAD)
            it1 = selt_ref[2, pl.ds(t, 1), :]
            lhs.append(jnp.where(sub_l == it0, 1.0, 0.0).astype(jnp.bfloat16))
            rhs.append(jnp.where(sub_r == it1, gt, 0.0).astype(jnp.bfloat16))
        lhs2 = jnp.concatenate(lhs, axis=1)
        rhs2 = jnp.concatenate([jnp.concatenate([rhs[0], zblk], axis=1),
                                jnp.concatenate([zblk, rhs[1]], axis=1)], axis=0)
        g2 = lax.dot_general(lhs2, rhs2, _NT, preferred_element_type=jnp.float32)
        base = pl.multiple_of(2 * tp * G_ROW_STRIDE, SUBLANES)
        gs_ref[pl.ds(base, nk), :] = g2[:nk, :nk]
        gs_ref[pl.ds(base + nk, mrows), :] = g2[:, nk:]

    half = topk // 2
    sub8 = lax.broadcasted_iota(jnp.int32, (half, tr), 0).astype(jnp.float32)
    sub16 = lax.broadcasted_iota(jnp.int32, (topk, tr), 0).astype(jnp.float32)
    code = lambda pos, a, b: pos * float(nk * nk) + a * float(nk) + b

    def level2(hd, v0, v1, i0, i1):
        cand = [v0[0:1] + v1]
        keys = [code(sub16, i0[0:1], i1)]
        for k0 in range(1, half):
            cand.append(v0[k0:k0 + 1] + v1[:half])
            keys.append(code(k0 * topk + sub8, i0[k0:k0 + 1], i1[:half]))
        cand.append(v0[half:] + v1[0:1])
        keys.append(code((half + sub8) * topk, i0[half:], i1[0:1]))
        tv, tk = _extract_topk(jnp.concatenate(cand, axis=0), jnp.concatenate(keys, axis=0), topk)
        e = jnp.exp(tv - tv[0:1])
        gate = e / jnp.sum(e, axis=0, keepdims=True)
        expert = tk.astype(jnp.int32) & (nk * nk - 1)
        rows = pl.ds(pl.multiple_of(hd * topk, topk), topk)
        sel_ref[0, rows, :] = gate
        sel_ref[1, rows, :] = (expert >> KEY_BITS).astype(jnp.float32)
        sel_ref[2, rows, :] = (expert & (nk - 1)).astype(jnp.float32)

    trips = nhp // TOPK_ILP
    pairs_per_trip = tr // 2 // trips

    def topk_and_onehot(j, tied):
        found = []
        for s in range(TOPK_ILP):
            v, ix, td = _topk_distinct(s_ref[TOPK_ILP * j + s], key_iota, topk)
            val_ref[TOPK_ILP * j + s] = v
            idx_ref[TOPK_ILP * j + s] = ix
            tied = jnp.maximum(tied, td)
            found.append((v, ix))
        for s in range(TOPK_ILP // 2):
            (v0, i0), (v1, i1) = found[2 * s], found[2 * s + 1]
            level2((TOPK_ILP // 2) * j + s, v0, v1, i0, i1)
        for u in range(pairs_per_trip):
            token_pair(j * pairs_per_trip + u)
        return tied

    tied = lax.fori_loop(0, trips, topk_and_onehot, jnp.zeros((SUBLANES, tr), jnp.float32))

    @pl.when(jnp.max(tied) > 0.0)
    def _():
        def level1_ties(n, _):
            v, ix = _extract_topk(s_ref[n], key_iota, topk)
            val_ref[n] = v
            idx_ref[n] = ix
            return 0

        lax.fori_loop(0, nhp, level1_ties, 0)

        def level2_ties(hd, _):
            level2(hd, val_ref[2 * hd], val_ref[2 * hd + 1], idx_ref[2 * hd], idx_ref[2 * hd + 1])
            return 0

        lax.fori_loop(0, PEER_HEADS, level2_ties, 0)

    for i in range(nk):
        gm_ref[:, i * nk:(i + 1) * nk] = gs_ref[pl.ds(i, tr, stride=G_ROW_STRIDE), :].astype(gm_ref.dtype)

    for a in range(3):
        selt_ref[a] = sel_ref[a].T


def _route(h, g, wq, sk):
    t, d = h.shape
    nq = wq.shape[1]
    nk = N_KEYS
    tr = LANES
    nblk = t // tr
    nhp = 2 * PEER_HEADS
    hk = PEER_HEADS * PEER_TOPK
    return pl.pallas_call(
        _route_kernel,
        grid=(nblk + 1,),
        in_specs=[pl.BlockSpec((tr, d), lambda i: (jnp.minimum(i, nblk - 1), 0)),
                  pl.BlockSpec((1, d), lambda i: (0, 0)),
                  pl.BlockSpec((d, nq), lambda i: (0, 0)),
                  pl.BlockSpec((2, nk, nk), lambda i: (0, 0, 0))],
        out_specs=pl.BlockSpec((tr, nk * nk), lambda i: (jnp.maximum(i - 1, 0), 0)),
        out_shape=jax.ShapeDtypeStruct((t, nk * nk), jnp.bfloat16),
        scratch_shapes=[pltpu.VMEM((nhp, nk, tr), jnp.float32),
                        pltpu.VMEM((nhp, PEER_TOPK, tr), jnp.float32),
                        pltpu.VMEM((nhp, PEER_TOPK, tr), jnp.float32),
                        pltpu.VMEM((3, hk, tr), jnp.float32),
                        pltpu.VMEM((3, tr, hk), jnp.float32),
                        pltpu.VMEM((tr * G_ROW_STRIDE, nk), jnp.float32)],
        compiler_params=_cparams("arbitrary"),
        name="peer_route",
    )(h, g, wq, sk)


def _peer_kernel(h_ref, g_ref, gm_ref, u_ref, v_ref, fg_ref, o_ref, y_ref, acc_ref, *, final_norm):
    e = pl.program_id(1)

    @pl.when(e == 0)
    def _():
        y_ref[...] = _rms(h_ref[...], g_ref[...]).astype(jnp.bfloat16)
        acc_ref[...] = jnp.zeros_like(acc_ref)

    hid = lax.dot_general(y_ref[...], u_ref[...].astype(jnp.bfloat16), _NT, preferred_element_type=jnp.float32)
    act = 0.5 * hid * (1.0 + lax.erf(hid * (1.0 / math.sqrt(2.0))))
    a = (gm_ref[...].astype(jnp.float32) * act).astype(jnp.bfloat16)
    acc_ref[...] += jnp.dot(a, v_ref[...].astype(jnp.bfloat16), preferred_element_type=jnp.float32)

    @pl.when(e == pl.num_programs(1) - 1)
    def _():
        out = h_ref[...] + acc_ref[...]
        if final_norm:
            out = _rms(out, fg_ref[...])
        o_ref[...] = out


def _peer(h, g, gm, u, v, layer, fg, tb, eb, final_norm):
    t, d = h.shape
    ne = v.shape[1]
    return pl.pallas_call(
        functools.partial(_peer_kernel, final_norm=final_norm),
        grid=(t // tb, ne // eb),
        in_specs=[pl.BlockSpec((tb, d), lambda i, e: (i, 0)),
                  pl.BlockSpec((1, d), lambda i, e: (0, 0)),
                  pl.BlockSpec((tb, eb), lambda i, e: (i, e)),
                  pl.BlockSpec((None, eb, d), lambda i, e: (layer, e, 0)),
                  pl.BlockSpec((None, eb, d), lambda i, e: (layer, e, 0)),
                  pl.BlockSpec((1, d), lambda i, e: (0, 0))],
        out_specs=pl.BlockSpec((tb, d), lambda i, e: (i, 0)),
        out_shape=jax.ShapeDtypeStruct((t, d), jnp.float32),
        scratch_shapes=[pltpu.VMEM((tb, d), jnp.bfloat16),
                        pltpu.VMEM((tb, d), jnp.float32)],
        compiler_params=_cparams("parallel", "arbitrary"),
        name="peer_experts",
    )(h, g, gm, u, v, fg)


def _seq_block(lp):
    return 3 * BLOCK_Q if lp % (3 * BLOCK_Q) == 0 else BLOCK_Q


def _forward(x, meta_tokens, mix_norm_g, ffn_norm_g, final_norm_g, ab_w_in, ab_conv_w,
             ab_w_out, cf_w_pw1, cf_b_pw1, cf_conv_w, cf_conv_b, cf_ln_g, cf_ln_b,
             cf_w_pw2, cf_b_pw2, peer_w_q, peer_sub_keys, peer_u, peer_v,
             peer_tb, peer_eb):
    bsz, s, d = x.shape
    l = s + N_META
    lp = -(-l // BLOCK_Q) * BLOCK_Q
    t = bsz * lp
    depth = mix_norm_g.shape[0]
    seq_blk = _seq_block(lp)
    bf = jnp.bfloat16
    row = lambda a: a.reshape(1, -1)

    h = jnp.concatenate([
        jnp.broadcast_to(meta_tokens.astype(x.dtype)[None], (bsz, N_META, d)),
        x,
        jnp.zeros((bsz, lp - l, d), x.dtype)], axis=1).reshape(t, d)

    a_width = A_HEADS * A_HEAD_DIM
    tri = (jnp.arange(BLOCK_Q)[:, None] > jnp.arange(BLOCK_Q)[None, :]).astype(bf)
    mhalf = jnp.concatenate([tri, jnp.ones((BLOCK_Q, BLOCK_Q), bf)], axis=1)
    mcat = jnp.concatenate([mhalf, mhalf], axis=0)
    qscale = jnp.concatenate([jnp.full((a_width,), 1.0 / math.sqrt(A_HEAD_DIM), jnp.float32),
                              jnp.ones((ab_w_in.shape[2] - a_width,), jnp.float32)])

    for i in range(depth):
        j = i // 2
        if i % 2 == 0:
            w_in = (ab_w_in[j] * qscale[None, :]).astype(bf)
            qkv, bb = _inproj(h, row(mix_norm_g[i]), w_in, 3 * a_width, seq_blk)
            a_out = _attention(qkv, mcat, bsz, lp, BLOCK_Q)
            h = _about(a_out, bb, ab_conv_w[j], ab_w_out[j].astype(bf), h, lp, seq_blk)
        else:
            z = _glu(h, row(mix_norm_g[i]), cf_w_pw1[j].astype(bf), row(cf_b_pw1[j]), seq_blk)
            h = _cfout(z, cf_conv_w[j], row(cf_conv_b[j]), row(cf_ln_g[j]), row(cf_ln_b[j]),
                       cf_w_pw2[j].astype(bf), row(cf_b_pw2[j]), h, lp, seq_blk)
        gm = _route(h, row(ffn_norm_g[i]), peer_w_q[i].astype(bf), peer_sub_keys[i].astype(bf))
        h = _peer(h, row(ffn_norm_g[i]), gm, peer_u, peer_v, i, row(final_norm_g),
                  peer_tb, peer_eb, final_norm=(i == depth - 1))
    return h.reshape(bsz, lp, d)[:, N_META:N_META + s, :]


def kernel(x, meta_tokens, mix_norm_g, ffn_norm_g, final_norm_g, ab_w_in, ab_conv_w, ab_w_out, cf_w_pw1, cf_b_pw1, cf_conv_w, cf_conv_b, cf_ln_g, cf_ln_b, cf_w_pw2, cf_b_pw2, peer_w_q, peer_sub_keys, peer_u, peer_v):
    return _forward(x, meta_tokens, mix_norm_g, ffn_norm_g, final_norm_g, ab_w_in, ab_conv_w,
                    ab_w_out, cf_w_pw1, cf_b_pw1, cf_conv_w, cf_conv_b, cf_ln_g, cf_ln_b,
                    cf_w_pw2, cf_b_pw2, peer_w_q, peer_sub_keys, peer_u, peer_v,
                    peer_tb=768, peer_eb=1024)
```

```python
import functools
import math

import jax
import jax.numpy as jnp
from jax import lax
from jax.experimental import pallas as pl
from jax.experimental.pallas import tpu as pltpu

N_META = 16
BLOCK_Q = 128
A_HEADS = 8
A_HEAD_DIM = 64
B_CONV = 3
CF_CONV = 31
PEER_HEADS = 8
N_KEYS = 128
PEER_TOPK = 16
EPS = 1e-6

LANES = 128
SUBLANES = 8
VMEM_LIMIT = 48 * 1024 * 1024

_NT = (((1,), (1,)), ((), ()))


def _cparams(*sem):
    return pltpu.CompilerParams(dimension_semantics=sem, vmem_limit_bytes=VMEM_LIMIT)


def _rms(x, g):
    return x * lax.rsqrt(jnp.mean(x * x, axis=-1, keepdims=True) + EPS) * g


def _inproj_kernel(h_ref, g_ref, w_ref, qkv_ref, bb_ref, *, a3):
    y = _rms(h_ref[...], g_ref[...]).astype(jnp.bfloat16)
    p = jnp.dot(y, w_ref[...], preferred_element_type=jnp.float32)
    qkv_ref[...] = p[:, :a3].astype(jnp.bfloat16)
    bb_ref[...] = p[:, a3:]


def _inproj(h, g, w, a3, tm):
    t, d = h.shape
    n = w.shape[1]
    return pl.pallas_call(
        functools.partial(_inproj_kernel, a3=a3),
        grid=(t // tm,),
        in_specs=[pl.BlockSpec((tm, d), lambda i: (i, 0)),
                  pl.BlockSpec((1, d), lambda i: (0, 0)),
                  pl.BlockSpec((d, n), lambda i: (0, 0))],
        out_specs=[pl.BlockSpec((tm, a3), lambda i: (i, 0)),
                   pl.BlockSpec((tm, n - a3), lambda i: (i, 0))],
        out_shape=[jax.ShapeDtypeStruct((t, a3), jnp.bfloat16),
                   jax.ShapeDtypeStruct((t, n - a3), jnp.float32)],
        compiler_params=_cparams("parallel"),
        name="ab_inproj",
    )(h, g, w)


def _glu_kernel(h_ref, g_ref, w_ref, b_ref, z_ref, *, inner):
    y = _rms(h_ref[...], g_ref[...]).astype(jnp.bfloat16)
    p = jnp.dot(y, w_ref[...], preferred_element_type=jnp.float32) + b_ref[...]
    z_ref[...] = p[:, :inner] * jax.nn.sigmoid(p[:, inner:])


def _glu(h, g, w, b, tm):
    t, d = h.shape
    n = w.shape[1]
    inner = n // 2
    return pl.pallas_call(
        functools.partial(_glu_kernel, inner=inner),
        grid=(t // tm,),
        in_specs=[pl.BlockSpec((tm, d), lambda i: (i, 0)),
                  pl.BlockSpec((1, d), lambda i: (0, 0)),
                  pl.BlockSpec((d, n), lambda i: (0, 0)),
                  pl.BlockSpec((1, n), lambda i: (0, 0))],
        out_specs=pl.BlockSpec((tm, inner), lambda i: (i, 0)),
        out_shape=jax.ShapeDtypeStruct((t, inner), jnp.float32),
        compiler_params=_cparams("parallel"),
        name="cf_glu",
    )(h, g, w, b)


F32_EXP_ZERO_BELOW = -104.0
_NEG_INF = float("-inf")


def _attn_kernel(q_ref, k_ref, v_ref, m_ref, o_ref, c_ref, acc_ref, *, bq, pairs, qb):
    g = pl.program_id(1)
    head0 = lax.broadcasted_iota(jnp.int32, (bq, LANES), 1) < A_HEAD_DIM
    row = lax.broadcasted_iota(jnp.int32, (bq, 2 * bq), 0)
    col = lax.broadcasted_iota(jnp.int32, (bq, 2 * bq), 1) & (bq - 1)
    causal = col < row
    mcat = m_ref[...]

    def split_heads(x):
        zero = jnp.zeros_like(x)
        return jnp.concatenate([jnp.where(head0, x, zero), jnp.where(head0, zero, x)], axis=0)

    def tile(step, first):
        chains = [(m, p) for m in range(qb) for p in range(pairs)]
        js = [g * qb + m - step for m in range(qb)]
        krows = [pl.ds(pl.multiple_of(jnp.maximum(j, 0) * bq, bq), bq) for j in js]
        lbs, lhs = [], []
        for m, p in chains:
            lanes = slice(p * LANES, (p + 1) * LANES)
            z = lax.dot_general(q_ref[m * bq:(m + 1) * bq, lanes], split_heads(k_ref[krows[m], lanes]), _NT,
                                preferred_element_type=jnp.float32)
            lb = jnp.minimum(z, 0.0) - jnp.log(1.0 + jnp.exp(-jnp.abs(z)))
            ln = lb - z
            if first:
                ln = jnp.where(causal, ln, 0.0)
            hi = ln.astype(jnp.bfloat16)
            lo = (ln - hi.astype(jnp.float32)).astype(jnp.bfloat16)
            lbs.append(lb)
            for h in range(2):
                lhs.append(jnp.concatenate([hi[:, h * bq:(h + 1) * bq], lo[:, h * bq:(h + 1) * bq]], axis=1))
        st = jnp.dot(jnp.concatenate(lhs, axis=0), mcat, preferred_element_type=jnp.float32)
        live = None
        for n, (m, p) in enumerate(chains):
            lanes = slice(p * LANES, (p + 1) * LANES)
            s0 = st[(2 * n) * bq:(2 * n + 1) * bq]
            s1 = st[(2 * n + 1) * bq:(2 * n + 2) * bq]
            suffix = jnp.concatenate([s0[:, :bq], s1[:, :bq]], axis=1)
            total = jnp.concatenate([s0[:, bq:], s1[:, bq:]], axis=1)
            if first:
                w = jnp.where(causal, jnp.exp(lbs[n] + suffix), 0.0)
                c_new = total
            else:
                c = c_ref[n]
                w = jnp.where(js[m] >= 0, jnp.exp(lbs[n] + suffix + c), 0.0)
                c_new = c + total
            pv = jnp.dot(w.astype(jnp.bfloat16), split_heads(v_ref[krows[m], lanes]),
                         preferred_element_type=jnp.float32)
            acc_ref[n] = pv if first else acc_ref[n] + pv
            c_ref[n] = c_new
            cm = jnp.where(js[m] >= 1, jnp.maximum(c_new[:, :bq], c_new[:, bq:]), _NEG_INF)
            live = cm if live is None else jnp.maximum(live, cm)
        return jnp.max(live)

    live0 = tile(0, True)

    def cond(st):
        return st[1] >= F32_EXP_ZERO_BELOW

    def body(st):
        step = st[0]
        return step + 1, tile(step, False)

    lax.while_loop(cond, body, (1, live0))
    for m in range(qb):
        o_ref[m * bq:(m + 1) * bq, :] = jnp.concatenate(
            [acc_ref[m * pairs + p] for p in range(pairs)], axis=1).astype(o_ref.dtype)


def _attention(qkv, mcat, batch, lp, bq):
    t = qkv.shape[0]
    aw = A_HEADS * A_HEAD_DIM
    pairs = aw // LANES
    nq = lp // bq
    qb = 3 if nq % 3 == 0 else 1
    ng = nq // qb
    return pl.pallas_call(
        functools.partial(_attn_kernel, bq=bq, pairs=pairs, qb=qb),
        grid=(batch, ng),
        in_specs=[pl.BlockSpec((qb * bq, aw), lambda b, i: (b * ng + i, 0)),
                  pl.BlockSpec((lp, aw), lambda b, i: (b, 1)),
                  pl.BlockSpec((lp, aw), lambda b, i: (b, 2)),
                  pl.BlockSpec((2 * bq, 2 * bq), lambda b, i: (0, 0))],
        out_specs=pl.BlockSpec((qb * bq, aw), lambda b, i: (b * ng + i, 0)),
        out_shape=jax.ShapeDtypeStruct((t, aw), jnp.bfloat16),
        scratch_shapes=[pltpu.VMEM((qb * pairs, bq, 2 * bq), jnp.float32),
                        pltpu.VMEM((qb * pairs, bq, LANES), jnp.float32)],
        compiler_params=_cparams("parallel", "arbitrary"),
        name="sb_attention",
    )(qkv, qkv, qkv, mcat)


def _about_kernel(a_ref, bb_ref, halo_ref, cw_ref, w_ref, h_ref, o_ref, *, nb, bw):
    i = pl.program_id(0)
    bb = bb_ref[...]
    gate_b, gate_c, hb = bb[:, :bw], bb[:, bw:2 * bw], bb[:, 2 * bw:]
    u = gate_c * hb
    hal = halo_ref[...]
    uh = hal[:, bw:2 * bw] * hal[:, 2 * bw:]
    uh = jnp.where(i % nb == 0, 0.0, uh)
    ucat = jnp.concatenate([uh, u], axis=0)
    tm = u.shape[0]
    cw = cw_ref[...]
    conv = u * cw[B_CONV - 1:B_CONV, :]
    for s in range(1, B_CONV):
        us = pltpu.roll(ucat, s, 0)[SUBLANES:SUBLANES + tm, :]
        conv = conv + us * cw[B_CONV - 1 - s:B_CONV - s, :]
    b_out = (gate_b * conv).astype(jnp.bfloat16)
    w = w_ref[...]
    aw = a_ref.shape[1]
    o_ref[...] = (h_ref[...]
                  + jnp.dot(a_ref[...], w[:aw, :], preferred_element_type=jnp.float32)
                  + jnp.dot(b_out, w[aw:, :], preferred_element_type=jnp.float32))


def _about(a_out, bb, conv_w, w_out, h, lp, tm):
    t, d = h.shape
    aw = a_out.shape[1]
    bw = conv_w.shape[1]
    nb = lp // tm
    hb = tm // SUBLANES
    return pl.pallas_call(
        functools.partial(_about_kernel, nb=nb, bw=bw),
        grid=(t // tm,),
        in_specs=[pl.BlockSpec((tm, aw), lambda i: (i, 0)),
                  pl.BlockSpec((tm, 3 * bw), lambda i: (i, 0)),
                  pl.BlockSpec((SUBLANES, 3 * bw), lambda i: (jnp.maximum(i * hb - 1, 0), 0)),
                  pl.BlockSpec((B_CONV, bw), lambda i: (0, 0)),
                  pl.BlockSpec((aw + bw, d), lambda i: (0, 0)),
                  pl.BlockSpec((tm, d), lambda i: (i, 0))],
        out_specs=pl.BlockSpec((tm, d), lambda i: (i, 0)),
        out_shape=jax.ShapeDtypeStruct((t, d), jnp.float32),
        compiler_params=_cparams("parallel"),
        name="ab_out",
    )(a_out, bb, bb, conv_w, w_out, h)


CF_HALO = 32


def _cfout_kernel(z_ref, halo_ref, cw_ref, cb_ref, lg_ref, lb_ref, w_ref, b_ref, h_ref, o_ref,
                  zs_ref, cv_ref, zp_ref, *, nb, rc):
    i = pl.program_id(0)
    tm, c = z_ref.shape
    zs_ref[0:CF_HALO, :] = jnp.where(i % nb == 0, 0.0, halo_ref[...])
    zs_ref[CF_HALO:, :] = z_ref[...]

    span = tm + CF_HALO - SUBLANES

    def conv_cols(j, _):
        cols = pl.ds(pl.multiple_of(j * LANES, LANES), LANES)
        cw = cw_ref[:, cols]
        cb = cb_ref[:, cols]
        for ph in range(1, SUBLANES):
            zp_ref[ph, 0:span, :] = zs_ref[ph:ph + span, cols]
        for r in range(0, tm, rc):
            acc = jnp.zeros((rc, LANES), jnp.float32) + cb
            for k in range(CF_CONV):
                off = r + CF_HALO - (CF_CONV - 1) + k
                ph = off % SUBLANES
                tap = zs_ref[off:off + rc, cols] if ph == 0 else zp_ref[ph, off - ph:off - ph + rc, :]
                acc = acc + tap * cw[k:k + 1, :]
            cv_ref[r:r + rc, cols] = acc
        return 0

    lax.fori_loop(0, c // LANES, conv_cols, 0)
    acc = cv_ref[...]
    mu = jnp.mean(acc, axis=-1, keepdims=True)
    xc = acc - mu
    var = jnp.mean(xc * xc, axis=-1, keepdims=True)
    y = xc * lax.rsqrt(var + EPS) * lg_ref[...] + lb_ref[...]
    y = y * jax.nn.sigmoid(y)
    o_ref[...] = (h_ref[...] + b_ref[...]
                  + jnp.dot(y.astype(jnp.bfloat16), w_ref[...], preferred_element_type=jnp.float32))


def _cfout(z, conv_w, conv_b, ln_g, ln_b, w2, b2, h, lp, tm):
    t, d = h.shape
    c = z.shape[1]
    nb = lp // tm
    hb = tm // CF_HALO
    vec = lambda n: pl.BlockSpec((1, n), lambda i: (0, 0))
    return pl.pallas_call(
        functools.partial(_cfout_kernel, nb=nb, rc=BLOCK_Q),
        grid=(t // tm,),
        in_specs=[pl.BlockSpec((tm, c), lambda i: (i, 0)),
                  pl.BlockSpec((CF_HALO, c), lambda i: (jnp.maximum(i * hb - 1, 0), 0)),
                  pl.BlockSpec((CF_CONV, c), lambda i: (0, 0)),
                  vec(c), vec(c), vec(c),
                  pl.BlockSpec((c, d), lambda i: (0, 0)),
                  vec(d),
                  pl.BlockSpec((tm, d), lambda i: (i, 0))],
        out_specs=pl.BlockSpec((tm, d), lambda i: (i, 0)),
        out_shape=jax.ShapeDtypeStruct((t, d), jnp.float32),
        scratch_shapes=[pltpu.VMEM((tm + CF_HALO, c), jnp.float32),
                        pltpu.VMEM((tm, c), jnp.float32),
                        pltpu.VMEM((SUBLANES, tm + CF_HALO, LANES), jnp.float32)],
        compiler_params=_cparams("parallel"),
        name="cf_out",
    )(z, z, conv_w, conv_b, ln_g, ln_b, w2, b2, h)


_NEG = float("-inf")
_BIGKEY = float(2 ** 30)
KEY_BITS = N_KEYS.bit_length() - 1
G_ROW_PAD = 4
G_ROW_STRIDE = N_KEYS + G_ROW_PAD
TOPK_ILP = 4


def _extract_topk(x, key, k):
    vals, keys = [], []
    for _ in range(k):
        m = jnp.max(x, axis=0, keepdims=True)
        km = jnp.min(jnp.where(x == m, key, _BIGKEY), axis=0, keepdims=True)
        x = jnp.where(key == km, _NEG, x)
        vals.append(m)
        keys.append(km)
    return jnp.concatenate(vals, axis=0), jnp.concatenate(keys, axis=0)


def _batcher_network(n):
    def merge(lo, hi, r):
        step = 2 * r
        if step < hi - lo:
            yield from merge(lo, hi, step)
            yield from merge(lo + r, hi, step)
            yield from ((i, i + r) for i in range(lo + r, hi - r, step))
        else:
            yield (lo, lo + r)

    def sort(lo, hi):
        if hi - lo >= 1:
            mid = lo + (hi - lo) // 2
            yield from sort(lo, mid)
            yield from sort(mid + 1, hi)
            yield from merge(lo, hi, 1)

    return tuple(sort(0, n - 1))


def _sublane_all(op, x):
    shift = SUBLANES // 2
    while shift:
        x = op(x, pltpu.roll(x, shift, 0))
        shift //= 2
    return x


def _topk_distinct(x, key, k):
    depth = x.shape[0] // SUBLANES
    vs = [x[d * SUBLANES:(d + 1) * SUBLANES] for d in range(depth)]
    ks = [key[d * SUBLANES:(d + 1) * SUBLANES] for d in range(depth)]
    for i, j in _batcher_network(depth):
        swap = vs[j] > vs[i]
        vs[i], vs[j] = jnp.maximum(vs[i], vs[j]), jnp.minimum(vs[i], vs[j])
        ks[i], ks[j] = jnp.where(swap, ks[j], ks[i]), jnp.where(swap, ks[i], ks[j])
    vals, keys = [], []
    tied = jnp.zeros(vs[0].shape, jnp.float32)
    prev = None
    for r in range(k + 1):
        m = _sublane_all(jnp.maximum, vs[0])
        if prev is not None:
            tied = jnp.maximum(tied, jnp.where(m >= prev, 1.0, 0.0))
        prev = m
        if r == k:
            break
        km = _sublane_all(jnp.minimum, jnp.where(vs[0] == m, ks[0], _BIGKEY))
        win = ks[0] == km
        vals.append(m[0:1])
        keys.append(km[0:1])
        for d in range(min(depth, k - r)):
            if d + 1 < depth:
                vs[d] = jnp.where(win, vs[d + 1], vs[d])
                ks[d] = jnp.where(win, ks[d + 1], ks[d])
            else:
                vs[d] = jnp.where(win, _NEG, vs[d])
    return jnp.concatenate(vals, axis=0), jnp.concatenate(keys, axis=0), tied


def _route_kernel(h_ref, g_ref, wq_ref, sk_ref, u_ref, v_ref, gm_ref, ub_ref, vb_ref,
                  s_ref, val_ref, idx_ref, sel_ref, selt_ref, gs_ref):
    step = pl.program_id(0)
    tr = h_ref.shape[0]
    nk = N_KEYS
    topk = PEER_TOPK
    nhp = 2 * PEER_HEADS

    @pl.when(step == 0)
    def _():
        selt_ref[...] = jnp.zeros_like(selt_ref)

    ub_ref[...] = u_ref[...].astype(ub_ref.dtype)
    vb_ref[...] = v_ref[...].astype(vb_ref.dtype)

    y = _rms(h_ref[...], g_ref[...]).astype(jnp.bfloat16)
    q = jnp.dot(y, wq_ref[...], preferred_element_type=jnp.float32).astype(jnp.bfloat16)
    for hp in range(nhp):
        s_ref[hp] = lax.dot_general(sk_ref[hp % 2], q[:, hp * nk:(hp + 1) * nk], _NT,
                                    preferred_element_type=jnp.float32)

    key_iota = lax.broadcasted_iota(jnp.int32, (nk, tr), 0).astype(jnp.float32)

    mrows = nk + 2 * G_ROW_PAD
    sub_l = lax.broadcasted_iota(jnp.int32, (mrows, LANES), 0).astype(jnp.float32)
    sub_r = lax.broadcasted_iota(jnp.int32, (nk, LANES), 0).astype(jnp.float32)
    zblk = jnp.zeros((nk, LANES), jnp.bfloat16)

    def token_pair(tp):
        lhs, rhs = [], []
        for s in range(2):
            t = 2 * tp + s
            gt = selt_ref[0, pl.ds(t, 1), :]
            it0 = selt_ref[1, pl.ds(t, 1), :] + float(s * G_ROW_PAD)
            it1 = selt_ref[2, pl.ds(t, 1), :]
            lhs.append(jnp.where(sub_l == it0, 1.0, 0.0).astype(jnp.bfloat16))
            rhs.append(jnp.where(sub_r == it1, gt, 0.0).astype(jnp.bfloat16))
        lhs2 = jnp.concatenate(lhs, axis=1)
        rhs2 = jnp.concatenate([jnp.concatenate([rhs[0], zblk], axis=1),
                                jnp.concatenate([zblk, rhs[1]], axis=1)], axis=0)
        g2 = lax.dot_general(lhs2, rhs2, _NT, preferred_element_type=jnp.float32)
        base = pl.multiple_of(2 * tp * G_ROW_STRIDE, SUBLANES)
        gs_ref[pl.ds(base, nk), :] = g2[:nk, :nk]
        gs_ref[pl.ds(base + nk, mrows), :] = g2[:, nk:]

    half = topk // 2
    sub8 = lax.broadcasted_iota(jnp.int32, (half, tr), 0).astype(jnp.float32)
    sub16 = lax.broadcasted_iota(jnp.int32, (topk, tr), 0).astype(jnp.float32)
    code = lambda pos, a, b: pos * float(nk * nk) + a * float(nk) + b

    def level2(hd, v0, v1, i0, i1):
        cand = [v0[0:1] + v1]
        keys = [code(sub16, i0[0:1], i1)]
        for k0 in range(1, half):
            cand.append(v0[k0:k0 + 1] + v1[:half])
            keys.append(code(k0 * topk + sub8, i0[k0:k0 + 1], i1[:half]))
        cand.append(v0[half:] + v1[0:1])
        keys.append(code((half + sub8) * topk, i0[half:], i1[0:1]))
        tv, tk = _extract_topk(jnp.concatenate(cand, axis=0), jnp.concatenate(keys, axis=0), topk)
        e = jnp.exp(tv - tv[0:1])
        gate = e / jnp.sum(e, axis=0, keepdims=True)
        expert = tk.astype(jnp.int32) & (nk * nk - 1)
        rows = pl.ds(pl.multiple_of(hd * topk, topk), topk)
        sel_ref[0, rows, :] = gate
        sel_ref[1, rows, :] = (expert >> KEY_BITS).astype(jnp.float32)
        sel_ref[2, rows, :] = (expert & (nk - 1)).astype(jnp.float32)

    trips = nhp // TOPK_ILP
    pairs_per_trip = tr // 2 // trips

    def topk_and_onehot(j, tied):
        found = []
        for s in range(TOPK_ILP):
            v, ix, td = _topk_distinct(s_ref[TOPK_ILP * j + s], key_iota, topk)
            val_ref[TOPK_ILP * j + s] = v
            idx_ref[TOPK_ILP * j + s] = ix
            tied = jnp.maximum(tied, td)
            found.append((v, ix))
        for s in range(TOPK_ILP // 2):
            (v0, i0), (v1, i1) = found[2 * s], found[2 * s + 1]
            level2((TOPK_ILP // 2) * j + s, v0, v1, i0, i1)
        for u in range(pairs_per_trip):
            token_pair(j * pairs_per_trip + u)
        return tied

    tied = lax.fori_loop(0, trips, topk_and_onehot, jnp.zeros((SUBLANES, tr), jnp.float32))

    @pl.when(jnp.max(tied) > 0.0)
    def _():
        def level1_ties(n, _):
            v, ix = _extract_topk(s_ref[n], key_iota, topk)
            val_ref[n] = v
            idx_ref[n] = ix
            return 0

        lax.fori_loop(0, nhp, level1_ties, 0)

        def level2_ties(hd, _):
            level2(hd, val_ref[2 * hd], val_ref[2 * hd + 1], idx_ref[2 * hd], idx_ref[2 * hd + 1])
            return 0

        lax.fori_loop(0, PEER_HEADS, level2_ties, 0)

    for i in range(nk):
        gm_ref[:, i * nk:(i + 1) * nk] = gs_ref[pl.ds(i, tr, stride=G_ROW_STRIDE), :].astype(gm_ref.dtype)

    for a in range(3):
        selt_ref[a] = sel_ref[a].T


def _route(h, g, wq, sk, u, v, layer):
    t, d = h.shape
    nq = wq.shape[1]
    nk = N_KEYS
    tr = LANES
    nblk = t // tr
    nhp = 2 * PEER_HEADS
    hk = PEER_HEADS * PEER_TOPK
    ne = u.shape[1]
    nslab = ne // tr
    assert ne % tr == 0 and nslab <= nblk + 1
    slab_in = pl.BlockSpec((None, tr, d), lambda i: (layer, jnp.minimum(i, nslab - 1), 0))
    slab_out = pl.BlockSpec((tr, d), lambda i: (jnp.minimum(i, nslab - 1), 0))
    return pl.pallas_call(
        _route_kernel,
        grid=(nblk + 1,),
        in_specs=[pl.BlockSpec((tr, d), lambda i: (jnp.minimum(i, nblk - 1), 0)),
                  pl.BlockSpec((1, d), lambda i: (0, 0)),
                  pl.BlockSpec((d, nq), lambda i: (0, 0)),
                  pl.BlockSpec((2, nk, nk), lambda i: (0, 0, 0)),
                  slab_in, slab_in],
        out_specs=[pl.BlockSpec((tr, nk * nk), lambda i: (jnp.maximum(i - 1, 0), 0)), slab_out, slab_out],
        out_shape=[jax.ShapeDtypeStruct((t, nk * nk), jnp.bfloat16),
                   jax.ShapeDtypeStruct((ne, d), jnp.bfloat16),
                   jax.ShapeDtypeStruct((ne, d), jnp.bfloat16)],
        scratch_shapes=[pltpu.VMEM((nhp, nk, tr), jnp.float32),
                        pltpu.VMEM((nhp, PEER_TOPK, tr), jnp.float32),
                        pltpu.VMEM((nhp, PEER_TOPK, tr), jnp.float32),
                        pltpu.VMEM((3, hk, tr), jnp.float32),
                        pltpu.VMEM((3, tr, hk), jnp.float32),
                        pltpu.VMEM((tr * G_ROW_STRIDE, nk), jnp.float32)],
        compiler_params=_cparams("arbitrary"),
        name="peer_route",
    )(h, g, wq, sk, u, v)


def _peer_kernel(h_ref, g_ref, gm_ref, u_ref, v_ref, fg_ref, o_ref, y_ref, acc_ref, *, final_norm):
    e = pl.program_id(1)

    @pl.when(e == 0)
    def _():
        y_ref[...] = _rms(h_ref[...], g_ref[...]).astype(jnp.bfloat16)
        acc_ref[...] = jnp.zeros_like(acc_ref)

    hid = lax.dot_general(y_ref[...], u_ref[...], _NT, preferred_element_type=jnp.float32)
    act = 0.5 * hid * (1.0 + lax.erf(hid * (1.0 / math.sqrt(2.0))))
    a = (gm_ref[...].astype(jnp.float32) * act).astype(jnp.bfloat16)
    acc_ref[...] += jnp.dot(a, v_ref[...], preferred_element_type=jnp.float32)

    @pl.when(e == pl.num_programs(1) - 1)
    def _():
        out = h_ref[...] + acc_ref[...]
        if final_norm:
            out = _rms(out, fg_ref[...])
        o_ref[...] = out


def _peer(h, g, gm, u, v, fg, tb, eb, final_norm):
    t, d = h.shape
    ne = v.shape[0]
    return pl.pallas_call(
        functools.partial(_peer_kernel, final_norm=final_norm),
        grid=(t // tb, ne // eb),
        in_specs=[pl.BlockSpec((tb, d), lambda i, e: (i, 0)),
                  pl.BlockSpec((1, d), lambda i, e: (0, 0)),
                  pl.BlockSpec((tb, eb), lambda i, e: (i, e)),
                  pl.BlockSpec((eb, d), lambda i, e: (e, 0)),
                  pl.BlockSpec((eb, d), lambda i, e: (e, 0)),
                  pl.BlockSpec((1, d), lambda i, e: (0, 0))],
        out_specs=pl.BlockSpec((tb, d), lambda i, e: (i, 0)),
        out_shape=jax.ShapeDtypeStruct((t, d), jnp.float32),
        scratch_shapes=[pltpu.VMEM((tb, d), jnp.bfloat16),
                        pltpu.VMEM((tb, d), jnp.float32)],
        compiler_params=_cparams("parallel", "arbitrary"),
        name="peer_experts",
    )(h, g, gm, u, v, fg)


def _seq_block(lp):
    return 3 * BLOCK_Q if lp % (3 * BLOCK_Q) == 0 else BLOCK_Q


def _forward(x, meta_tokens, mix_norm_g, ffn_norm_g, final_norm_g, ab_w_in, ab_conv_w,
             ab_w_out, cf_w_pw1, cf_b_pw1, cf_conv_w, cf_conv_b, cf_ln_g, cf_ln_b,
             cf_w_pw2, cf_b_pw2, peer_w_q, peer_sub_keys, peer_u, peer_v,
             peer_tb, peer_eb):
    bsz, s, d = x.shape
    l = s + N_META
    lp = -(-l // BLOCK_Q) * BLOCK_Q
    t = bsz * lp
    depth = mix_norm_g.shape[0]
    seq_blk = _seq_block(lp)
    bf = jnp.bfloat16
    row = lambda a: a.reshape(1, -1)

    h = jnp.concatenate([
        jnp.broadcast_to(meta_tokens.astype(x.dtype)[None], (bsz, N_META, d)),
        x,
        jnp.zeros((bsz, lp - l, d), x.dtype)], axis=1).reshape(t, d)

    a_width = A_HEADS * A_HEAD_DIM
    tri = (jnp.arange(BLOCK_Q)[:, None] > jnp.arange(BLOCK_Q)[None, :]).astype(bf)
    mhalf = jnp.concatenate([tri, jnp.ones((BLOCK_Q, BLOCK_Q), bf)], axis=1)
    mcat = jnp.concatenate([mhalf, mhalf], axis=0)
    qscale = jnp.concatenate([jnp.full((a_width,), 1.0 / math.sqrt(A_HEAD_DIM), jnp.float32),
                              jnp.ones((ab_w_in.shape[2] - a_width,), jnp.float32)])

    for i in range(depth):
        j = i // 2
        if i % 2 == 0:
            w_in = (ab_w_in[j] * qscale[None, :]).astype(bf)
            qkv, bb = _inproj(h, row(mix_norm_g[i]), w_in, 3 * a_width, seq_blk)
            a_out = _attention(qkv, mcat, bsz, lp, BLOCK_Q)
            h = _about(a_out, bb, ab_conv_w[j], ab_w_out[j].astype(bf), h, lp, seq_blk)
        else:
            z = _glu(h, row(mix_norm_g[i]), cf_w_pw1[j].astype(bf), row(cf_b_pw1[j]), seq_blk)
            h = _cfout(z, cf_conv_w[j], row(cf_conv_b[j]), row(cf_ln_g[j]), row(cf_ln_b[j]),
                       cf_w_pw2[j].astype(bf), row(cf_b_pw2[j]), h, lp, seq_blk)
        gm, u_bf, v_bf = _route(h, row(ffn_norm_g[i]), peer_w_q[i].astype(bf), peer_sub_keys[i].astype(bf),
                                peer_u, peer_v, i)
        h = _peer(h, row(ffn_norm_g[i]), gm, u_bf, v_bf, row(final_norm_g),
                  peer_tb, peer_eb, final_norm=(i == depth - 1))
    return h.reshape(bsz, lp, d)[:, N_META:N_META + s, :]


def kernel(x, meta_tokens, mix_norm_g, ffn_norm_g, final_norm_g, ab_w_in, ab_conv_w, ab_w_out, cf_w_pw1, cf_b_pw1, cf_conv_w, cf_conv_b, cf_ln_g, cf_ln_b, cf_w_pw2, cf_b_pw2, peer_w_q, peer_sub_keys, peer_u, peer_v):
    return _forward(x, meta_tokens, mix_norm_g, ffn_norm_g, final_norm_g, ab_w_in, ab_conv_w,
                    ab_w_out, cf_w_pw1, cf_b_pw1, cf_conv_w, cf_conv_b, cf_ln_g, cf_ln_b,
                    cf_w_pw2, cf_b_pw2, peer_w_q, peer_sub_keys, peer_u, peer_v,
                    peer_tb=768, peer_eb=1024)
```

```python
import functools
import math

import jax
import jax.numpy as jnp
from jax import lax
from jax.experimental import pallas as pl
from jax.experimental.pallas import tpu as pltpu

N_META = 16
BLOCK_Q = 128
A_HEADS = 8
A_HEAD_DIM = 64
B_CONV = 3
CF_CONV = 31
PEER_HEADS = 8
N_KEYS = 128
PEER_TOPK = 16
EPS = 1e-6

LANES = 128
SUBLANES = 8
VMEM_LIMIT = 48 * 1024 * 1024

_NT = (((1,), (1,)), ((), ()))


def _cparams(*sem):
    return pltpu.CompilerParams(dimension_semantics=sem, vmem_limit_bytes=VMEM_LIMIT)


def _rms(x, g):
    return x * lax.rsqrt(jnp.mean(x * x, axis=-1, keepdims=True) + EPS) * g


def _inproj_kernel(h_ref, g_ref, w_ref, qkv_ref, bb_ref, *, a3):
    y = _rms(h_ref[...], g_ref[...]).astype(jnp.bfloat16)
    p = jnp.dot(y, w_ref[...], preferred_element_type=jnp.float32)
    qkv_ref[...] = p[:, :a3].astype(jnp.bfloat16)
    bb_ref[...] = p[:, a3:]


def _inproj(h, g, w, a3, tm):
    t, d = h.shape
    n = w.shape[1]
    return pl.pallas_call(
        functools.partial(_inproj_kernel, a3=a3),
        grid=(t // tm,),
        in_specs=[pl.BlockSpec((tm, d), lambda i: (i, 0)),
                  pl.BlockSpec((1, d), lambda i: (0, 0)),
                  pl.BlockSpec((d, n), lambda i: (0, 0))],
        out_specs=[pl.BlockSpec((tm, a3), lambda i: (i, 0)),
                   pl.BlockSpec((tm, n - a3), lambda i: (i, 0))],
        out_shape=[jax.ShapeDtypeStruct((t, a3), jnp.bfloat16),
                   jax.ShapeDtypeStruct((t, n - a3), jnp.float32)],
        compiler_params=_cparams("parallel"),
        name="ab_inproj",
    )(h, g, w)


def _glu_kernel(h_ref, g_ref, w_ref, b_ref, z_ref, *, inner):
    y = _rms(h_ref[...], g_ref[...]).astype(jnp.bfloat16)
    p = jnp.dot(y, w_ref[...], preferred_element_type=jnp.float32) + b_ref[...]
    z_ref[...] = p[:, :inner] * jax.nn.sigmoid(p[:, inner:])


def _glu(h, g, w, b, tm):
    t, d = h.shape
    n = w.shape[1]
    inner = n // 2
    return pl.pallas_call(
        functools.partial(_glu_kernel, inner=inner),
        grid=(t // tm,),
        in_specs=[pl.BlockSpec((tm, d), lambda i: (i, 0)),
                  pl.BlockSpec((1, d), lambda i: (0, 0)),
                  pl.BlockSpec((d, n), lambda i: (0, 0)),
                  pl.BlockSpec((1, n), lambda i: (0, 0))],
        out_specs=pl.BlockSpec((tm, inner), lambda i: (i, 0)),
        out_shape=jax.ShapeDtypeStruct((t, inner), jnp.float32),
        compiler_params=_cparams("parallel"),
        name="cf_glu",
    )(h, g, w, b)


F32_EXP_ZERO_BELOW = -104.0
_NEG_INF = float("-inf")


def _attn_kernel(q_ref, k_ref, v_ref, m_ref, o_ref, c_ref, acc_ref, *, bq, pairs, qb):
    g = pl.program_id(1)
    head0 = lax.broadcasted_iota(jnp.int32, (bq, LANES), 1) < A_HEAD_DIM
    row = lax.broadcasted_iota(jnp.int32, (bq, 2 * bq), 0)
    col = lax.broadcasted_iota(jnp.int32, (bq, 2 * bq), 1) & (bq - 1)
    causal = col < row
    mcat = m_ref[...]

    def split_heads(x):
        zero = jnp.zeros_like(x)
        return jnp.concatenate([jnp.where(head0, x, zero), jnp.where(head0, zero, x)], axis=0)

    def tile(step, first):
        chains = [(m, p) for m in range(qb) for p in range(pairs)]
        js = [g * qb + m - step for m in range(qb)]
        krows = [pl.ds(pl.multiple_of(jnp.maximum(j, 0) * bq, bq), bq) for j in js]
        lbs, lhs = [], []
        for m, p in chains:
            lanes = slice(p * LANES, (p + 1) * LANES)
            z = lax.dot_general(q_ref[m * bq:(m + 1) * bq, lanes], split_heads(k_ref[krows[m], lanes]), _NT,
                                preferred_element_type=jnp.float32)
            lb = jnp.minimum(z, 0.0) - jnp.log(1.0 + jnp.exp(-jnp.abs(z)))
            ln = lb - z
            if first:
                ln = jnp.where(causal, ln, 0.0)
            hi = ln.astype(jnp.bfloat16)
            lo = (ln - hi.astype(jnp.float32)).astype(jnp.bfloat16)
            lbs.append(lb)
            for h in range(2):
                lhs.append(jnp.concatenate([hi[:, h * bq:(h + 1) * bq], lo[:, h * bq:(h + 1) * bq]], axis=1))
        st = jnp.dot(jnp.concatenate(lhs, axis=0), mcat, preferred_element_type=jnp.float32)
        live = None
        for n, (m, p) in enumerate(chains):
            lanes = slice(p * LANES, (p + 1) * LANES)
            s0 = st[(2 * n) * bq:(2 * n + 1) * bq]
            s1 = st[(2 * n + 1) * bq:(2 * n + 2) * bq]
            suffix = jnp.concatenate([s0[:, :bq], s1[:, :bq]], axis=1)
            total = jnp.concatenate([s0[:, bq:], s1[:, bq:]], axis=1)
            if first:
                w = jnp.where(causal, jnp.exp(lbs[n] + suffix), 0.0)
                c_new = total
            else:
                c = c_ref[n]
                w = jnp.where(js[m] >= 0, jnp.exp(lbs[n] + suffix + c), 0.0)
                c_new = c + total
            pv = jnp.dot(w.astype(jnp.bfloat16), split_heads(v_ref[krows[m], lanes]),
                         preferred_element_type=jnp.float32)
            acc_ref[n] = pv if first else acc_ref[n] + pv
            c_ref[n] = c_new
            cm = jnp.where(js[m] >= 1, jnp.maximum(c_new[:, :bq], c_new[:, bq:]), _NEG_INF)
            live = cm if live is None else jnp.maximum(live, cm)
        return jnp.max(live)

    live0 = tile(0, True)

    def cond(st):
        return st[1] >= F32_EXP_ZERO_BELOW

    def body(st):
        step = st[0]
        return step + 1, tile(step, False)

    lax.while_loop(cond, body, (1, live0))
    for m in range(qb):
        o_ref[m * bq:(m + 1) * bq, :] = jnp.concatenate(
            [acc_ref[m * pairs + p] for p in range(pairs)], axis=1).astype(o_ref.dtype)


def _attention(qkv, mcat, batch, lp, bq):
    t = qkv.shape[0]
    aw = A_HEADS * A_HEAD_DIM
    pairs = aw // LANES
    nq = lp // bq
    qb = 3 if nq % 3 == 0 else 1
    ng = nq // qb
    return pl.pallas_call(
        functools.partial(_attn_kernel, bq=bq, pairs=pairs, qb=qb),
        grid=(batch, ng),
        in_specs=[pl.BlockSpec((qb * bq, aw), lambda b, i: (b * ng + i, 0)),
                  pl.BlockSpec((lp, aw), lambda b, i: (b, 1)),
                  pl.BlockSpec((lp, aw), lambda b, i: (b, 2)),
                  pl.BlockSpec((2 * bq, 2 * bq), lambda b, i: (0, 0))],
        out_specs=pl.BlockSpec((qb * bq, aw), lambda b, i: (b * ng + i, 0)),
        out_shape=jax.ShapeDtypeStruct((t, aw), jnp.bfloat16),
        scratch_shapes=[pltpu.VMEM((qb * pairs, bq, 2 * bq), jnp.float32),
                        pltpu.VMEM((qb * pairs, bq, LANES), jnp.float32)],
        compiler_params=_cparams("parallel", "arbitrary"),
        name="sb_attention",
    )(qkv, qkv, qkv, mcat)


def _about_kernel(a_ref, bb_ref, halo_ref, cw_ref, w_ref, h_ref, o_ref, *, nb, bw):
    i = pl.program_id(0)
    bb = bb_ref[...]
    gate_b, gate_c, hb = bb[:, :bw], bb[:, bw:2 * bw], bb[:, 2 * bw:]
    u = gate_c * hb
    hal = halo_ref[...]
    uh = hal[:, bw:2 * bw] * hal[:, 2 * bw:]
    uh = jnp.where(i % nb == 0, 0.0, uh)
    ucat = jnp.concatenate([uh, u], axis=0)
    tm = u.shape[0]
    cw = cw_ref[...]
    conv = u * cw[B_CONV - 1:B_CONV, :]
    for s in range(1, B_CONV):
        us = pltpu.roll(ucat, s, 0)[SUBLANES:SUBLANES + tm, :]
        conv = conv + us * cw[B_CONV - 1 - s:B_CONV - s, :]
    b_out = (gate_b * conv).astype(jnp.bfloat16)
    w = w_ref[...]
    aw = a_ref.shape[1]
    o_ref[...] = (h_ref[...]
                  + jnp.dot(a_ref[...], w[:aw, :], preferred_element_type=jnp.float32)
                  + jnp.dot(b_out, w[aw:, :], preferred_element_type=jnp.float32))


def _about(a_out, bb, conv_w, w_out, h, lp, tm):
    t, d = h.shape
    aw = a_out.shape[1]
    bw = conv_w.shape[1]
    nb = lp // tm
    hb = tm // SUBLANES
    return pl.pallas_call(
        functools.partial(_about_kernel, nb=nb, bw=bw),
        grid=(t // tm,),
        in_specs=[pl.BlockSpec((tm, aw), lambda i: (i, 0)),
                  pl.BlockSpec((tm, 3 * bw), lambda i: (i, 0)),
                  pl.BlockSpec((SUBLANES, 3 * bw), lambda i: (jnp.maximum(i * hb - 1, 0), 0)),
                  pl.BlockSpec((B_CONV, bw), lambda i: (0, 0)),
                  pl.BlockSpec((aw + bw, d), lambda i: (0, 0)),
                  pl.BlockSpec((tm, d), lambda i: (i, 0))],
        out_specs=pl.BlockSpec((tm, d), lambda i: (i, 0)),
        out_shape=jax.ShapeDtypeStruct((t, d), jnp.float32),
        compiler_params=_cparams("parallel"),
        name="ab_out",
    )(a_out, bb, bb, conv_w, w_out, h)


CF_HALO = 32


def _cfout_kernel(z_ref, halo_ref, cw_ref, cb_ref, lg_ref, lb_ref, w_ref, b_ref, h_ref, o_ref,
                  zs_ref, cv_ref, zp_ref, *, nb, rc):
    i = pl.program_id(0)
    tm, c = z_ref.shape
    zs_ref[0:CF_HALO, :] = jnp.where(i % nb == 0, 0.0, halo_ref[...])
    zs_ref[CF_HALO:, :] = z_ref[...]

    span = tm + CF_HALO - SUBLANES

    def conv_cols(j, _):
        cols = pl.ds(pl.multiple_of(j * LANES, LANES), LANES)
        cw = cw_ref[:, cols]
        cb = cb_ref[:, cols]
        for ph in range(1, SUBLANES):
            zp_ref[ph, 0:span, :] = zs_ref[ph:ph + span, cols]
        for r in range(0, tm, rc):
            acc = jnp.zeros((rc, LANES), jnp.float32) + cb
            for k in range(CF_CONV):
                off = r + CF_HALO - (CF_CONV - 1) + k
                ph = off % SUBLANES
                tap = zs_ref[off:off + rc, cols] if ph == 0 else zp_ref[ph, off - ph:off - ph + rc, :]
                acc = acc + tap * cw[k:k + 1, :]
            cv_ref[r:r + rc, cols] = acc
        return 0

    lax.fori_loop(0, c // LANES, conv_cols, 0)
    acc = cv_ref[...]
    mu = jnp.mean(acc, axis=-1, keepdims=True)
    xc = acc - mu
    var = jnp.mean(xc * xc, axis=-1, keepdims=True)
    y = xc * lax.rsqrt(var + EPS) * lg_ref[...] + lb_ref[...]
    y = y * jax.nn.sigmoid(y)
    o_ref[...] = (h_ref[...] + b_ref[...]
                  + jnp.dot(y.astype(jnp.bfloat16), w_ref[...], preferred_element_type=jnp.float32))


def _cfout(z, conv_w, conv_b, ln_g, ln_b, w2, b2, h, lp, tm):
    t, d = h.shape
    c = z.shape[1]
    nb = lp // tm
    hb = tm // CF_HALO
    vec = lambda n: pl.BlockSpec((1, n), lambda i: (0, 0))
    return pl.pallas_call(
        functools.partial(_cfout_kernel, nb=nb, rc=BLOCK_Q),
        grid=(t // tm,),
        in_specs=[pl.BlockSpec((tm, c), lambda i: (i, 0)),
                  pl.BlockSpec((CF_HALO, c), lambda i: (jnp.maximum(i * hb - 1, 0), 0)),
                  pl.BlockSpec((CF_CONV, c), lambda i: (0, 0)),
                  vec(c), vec(c), vec(c),
                  pl.BlockSpec((c, d), lambda i: (0, 0)),
                  vec(d),
                  pl.BlockSpec((tm, d), lambda i: (i, 0))],
        out_specs=pl.BlockSpec((tm, d), lambda i: (i, 0)),
        out_shape=jax.ShapeDtypeStruct((t, d), jnp.float32),
        scratch_shapes=[pltpu.VMEM((tm + CF_HALO, c), jnp.float32),
                        pltpu.VMEM((tm, c), jnp.float32),
                        pltpu.VMEM((SUBLANES, tm + CF_HALO, LANES), jnp.float32)],
        compiler_params=_cparams("parallel"),
        name="cf_out",
    )(z, z, conv_w, conv_b, ln_g, ln_b, w2, b2, h)


_NEG = float("-inf")
_BIGKEY = float(2 ** 30)
KEY_BITS = N_KEYS.bit_length() - 1
G_ROW_PAD = 4
G_ROW_STRIDE = N_KEYS + G_ROW_PAD
TOPK_ILP = 4


def _extract_topk(x, key, k):
    vals, keys = [], []
    for _ in range(k):
        m = jnp.max(x, axis=0, keepdims=True)
        km = jnp.min(jnp.where(x == m, key, _BIGKEY), axis=0, keepdims=True)
        x = jnp.where(key == km, _NEG, x)
        vals.append(m)
        keys.append(km)
    return jnp.concatenate(vals, axis=0), jnp.concatenate(keys, axis=0)


def _batcher_network(n):
    def merge(lo, hi, r):
        step = 2 * r
        if step < hi - lo:
            yield from merge(lo, hi, step)
            yield from merge(lo + r, hi, step)
            yield from ((i, i + r) for i in range(lo + r, hi - r, step))
        else:
            yield (lo, lo + r)

    def sort(lo, hi):
        if hi - lo >= 1:
            mid = lo + (hi - lo) // 2
            yield from sort(lo, mid)
            yield from sort(mid + 1, hi)
            yield from merge(lo, hi, 1)

    return tuple(sort(0, n - 1))


def _sublane_all(op, x):
    shift = SUBLANES // 2
    while shift:
        x = op(x, pltpu.roll(x, shift, 0))
        shift //= 2
    return x


def _topk_distinct(x, key, k):
    depth = x.shape[0] // SUBLANES
    vs = [x[d * SUBLANES:(d + 1) * SUBLANES] for d in range(depth)]
    ks = [key[d * SUBLANES:(d + 1) * SUBLANES] for d in range(depth)]
    for i, j in _batcher_network(depth):
        swap = vs[j] > vs[i]
        vs[i], vs[j] = jnp.maximum(vs[i], vs[j]), jnp.minimum(vs[i], vs[j])
        ks[i], ks[j] = jnp.where(swap, ks[j], ks[i]), jnp.where(swap, ks[i], ks[j])
    vals, keys = [], []
    tied = jnp.zeros(vs[0].shape, jnp.float32)
    prev = None
    for r in range(k + 1):
        m = _sublane_all(jnp.maximum, vs[0])
        if prev is not None:
            tied = jnp.maximum(tied, jnp.where(m >= prev, 1.0, 0.0))
        prev = m
        if r == k:
            break
        km = _sublane_all(jnp.minimum, jnp.where(vs[0] == m, ks[0], _BIGKEY))
        win = ks[0] == km
        vals.append(m[0:1])
        keys.append(km[0:1])
        for d in range(min(depth, k - r)):
            if d + 1 < depth:
                vs[d] = jnp.where(win, vs[d + 1], vs[d])
                ks[d] = jnp.where(win, ks[d + 1], ks[d])
            else:
                vs[d] = jnp.where(win, _NEG, vs[d])
    return jnp.concatenate(vals, axis=0), jnp.concatenate(keys, axis=0), tied


def _route_kernel(h_ref, g_ref, wq_ref, sk_ref, u_ref, v_ref, gm_ref, ub_ref, vb_ref, yb_ref,
                  s_ref, val_ref, idx_ref, sel_ref, selt_ref, gs_ref):
    step = pl.program_id(0)
    tr = h_ref.shape[0]
    nk = N_KEYS
    topk = PEER_TOPK
    nhp = 2 * PEER_HEADS

    @pl.when(step == 0)
    def _():
        selt_ref[...] = jnp.zeros_like(selt_ref)

    ub_ref[...] = u_ref[...].astype(ub_ref.dtype)
    vb_ref[...] = v_ref[...].astype(vb_ref.dtype)

    y = _rms(h_ref[...], g_ref[...]).astype(jnp.bfloat16)
    yb_ref[...] = y
    q = jnp.dot(y, wq_ref[...], preferred_element_type=jnp.float32).astype(jnp.bfloat16)
    for hp in range(nhp):
        s_ref[hp] = lax.dot_general(sk_ref[hp % 2], q[:, hp * nk:(hp + 1) * nk], _NT,
                                    preferred_element_type=jnp.float32)

    key_iota = lax.broadcasted_iota(jnp.int32, (nk, tr), 0).astype(jnp.float32)

    mrows = nk + 2 * G_ROW_PAD
    sub_l = lax.broadcasted_iota(jnp.int32, (mrows, LANES), 0).astype(jnp.float32)
    sub_r = lax.broadcasted_iota(jnp.int32, (nk, LANES), 0).astype(jnp.float32)
    zblk = jnp.zeros((nk, LANES), jnp.bfloat16)

    def token_pair(tp):
        lhs, rhs = [], []
        for s in range(2):
            t = 2 * tp + s
            gt = selt_ref[0, pl.ds(t, 1), :]
            it0 = selt_ref[1, pl.ds(t, 1), :] + float(s * G_ROW_PAD)
            it1 = selt_ref[2, pl.ds(t, 1), :]
            lhs.append(jnp.where(sub_l == it0, 1.0, 0.0).astype(jnp.bfloat16))
            rhs.append(jnp.where(sub_r == it1, gt, 0.0).astype(jnp.bfloat16))
        lhs2 = jnp.concatenate(lhs, axis=1)
        rhs2 = jnp.concatenate([jnp.concatenate([rhs[0], zblk], axis=1),
                                jnp.concatenate([zblk, rhs[1]], axis=1)], axis=0)
        g2 = lax.dot_general(lhs2, rhs2, _NT, preferred_element_type=jnp.float32)
        base = pl.multiple_of(2 * tp * G_ROW_STRIDE, SUBLANES)
        gs_ref[pl.ds(base, nk), :] = g2[:nk, :nk]
        gs_ref[pl.ds(base + nk, mrows), :] = g2[:, nk:]

    half = topk // 2
    sub8 = lax.broadcasted_iota(jnp.int32, (half, tr), 0).astype(jnp.float32)
    sub16 = lax.broadcasted_iota(jnp.int32, (topk, tr), 0).astype(jnp.float32)
    code = lambda pos, a, b: pos * float(nk * nk) + a * float(nk) + b

    def level2(hd, v0, v1, i0, i1):
        cand = [v0[0:1] + v1]
        keys = [code(sub16, i0[0:1], i1)]
        for k0 in range(1, half):
            cand.append(v0[k0:k0 + 1] + v1[:half])
            keys.append(code(k0 * topk + sub8, i0[k0:k0 + 1], i1[:half]))
        cand.append(v0[half:] + v1[0:1])
        keys.append(code((half + sub8) * topk, i0[half:], i1[0:1]))
        tv, tk = _extract_topk(jnp.concatenate(cand, axis=0), jnp.concatenate(keys, axis=0), topk)
        e = jnp.exp(tv - tv[0:1])
        gate = e / jnp.sum(e, axis=0, keepdims=True)
        expert = tk.astype(jnp.int32) & (nk * nk - 1)
        rows = pl.ds(pl.multiple_of(hd * topk, topk), topk)
        sel_ref[0, rows, :] = gate
        sel_ref[1, rows, :] = (expert >> KEY_BITS).astype(jnp.float32)
        sel_ref[2, rows, :] = (expert & (nk - 1)).astype(jnp.float32)

    trips = nhp // TOPK_ILP
    pairs_per_trip = tr // 2 // trips

    def topk_and_onehot(j, tied):
        found = []
        for s in range(TOPK_ILP):
            v, ix, td = _topk_distinct(s_ref[TOPK_ILP * j + s], key_iota, topk)
            val_ref[TOPK_ILP * j + s] = v
            idx_ref[TOPK_ILP * j + s] = ix
            tied = jnp.maximum(tied, td)
            found.append((v, ix))
        for s in range(TOPK_ILP // 2):
            (v0, i0), (v1, i1) = found[2 * s], found[2 * s + 1]
            level2((TOPK_ILP // 2) * j + s, v0, v1, i0, i1)
        for u in range(pairs_per_trip):
            token_pair(j * pairs_per_trip + u)
        return tied

    tied = lax.fori_loop(0, trips, topk_and_onehot, jnp.zeros((SUBLANES, tr), jnp.float32))

    @pl.when(jnp.max(tied) > 0.0)
    def _():
        def level1_ties(n, _):
            v, ix = _extract_topk(s_ref[n], key_iota, topk)
            val_ref[n] = v
            idx_ref[n] = ix
            return 0

        lax.fori_loop(0, nhp, level1_ties, 0)

        def level2_ties(hd, _):
            level2(hd, val_ref[2 * hd], val_ref[2 * hd + 1], idx_ref[2 * hd], idx_ref[2 * hd + 1])
            return 0

        lax.fori_loop(0, PEER_HEADS, level2_ties, 0)

    for i in range(nk):
        gm_ref[:, i * nk:(i + 1) * nk] = gs_ref[pl.ds(i, tr, stride=G_ROW_STRIDE), :].astype(gm_ref.dtype)

    for a in range(3):
        selt_ref[a] = sel_ref[a].T


def _route(h, g, wq, sk, u, v, layer):
    t, d = h.shape
    nq = wq.shape[1]
    nk = N_KEYS
    tr = LANES
    nblk = t // tr
    nhp = 2 * PEER_HEADS
    hk = PEER_HEADS * PEER_TOPK
    ne = u.shape[1]
    nslab = ne // tr
    assert ne % tr == 0 and nslab <= nblk + 1
    slab_in = pl.BlockSpec((None, tr, d), lambda i: (layer, jnp.minimum(i, nslab - 1), 0))
    slab_out = pl.BlockSpec((tr, d), lambda i: (jnp.minimum(i, nslab - 1), 0))
    return pl.pallas_call(
        _route_kernel,
        grid=(nblk + 1,),
        in_specs=[pl.BlockSpec((tr, d), lambda i: (jnp.minimum(i, nblk - 1), 0)),
                  pl.BlockSpec((1, d), lambda i: (0, 0)),
                  pl.BlockSpec((d, nq), lambda i: (0, 0)),
                  pl.BlockSpec((2, nk, nk), lambda i: (0, 0, 0)),
                  slab_in, slab_in],
        out_specs=[pl.BlockSpec((tr, nk * nk), lambda i: (jnp.maximum(i - 1, 0), 0)), slab_out, slab_out,
                   pl.BlockSpec((tr, d), lambda i: (jnp.minimum(i, nblk - 1), 0))],
        out_shape=[jax.ShapeDtypeStruct((t, nk * nk), jnp.bfloat16),
                   jax.ShapeDtypeStruct((ne, d), jnp.bfloat16),
                   jax.ShapeDtypeStruct((ne, d), jnp.bfloat16),
                   jax.ShapeDtypeStruct((t, d), jnp.bfloat16)],
        scratch_shapes=[pltpu.VMEM((nhp, nk, tr), jnp.float32),
                        pltpu.VMEM((nhp, PEER_TOPK, tr), jnp.float32),
                        pltpu.VMEM((nhp, PEER_TOPK, tr), jnp.float32),
                        pltpu.VMEM((3, hk, tr), jnp.float32),
                        pltpu.VMEM((3, tr, hk), jnp.float32),
                        pltpu.VMEM((tr * G_ROW_STRIDE, nk), jnp.float32)],
        compiler_params=_cparams("arbitrary"),
        name="peer_route",
    )(h, g, wq, sk, u, v)


def _peer_kernel(h_ref, y_ref, gm_ref, u_ref, v_ref, fg_ref, o_ref, acc_ref, *, final_norm):
    e = pl.program_id(1)

    @pl.when(e == 0)
    def _():
        acc_ref[...] = jnp.zeros_like(acc_ref)

    hid = lax.dot_general(y_ref[...], u_ref[...], _NT, preferred_element_type=jnp.float32)
    act = 0.5 * hid * (1.0 + lax.erf(hid * (1.0 / math.sqrt(2.0))))
    a = (gm_ref[...].astype(jnp.float32) * act).astype(jnp.bfloat16)
    acc_ref[...] += jnp.dot(a, v_ref[...], preferred_element_type=jnp.float32)

    @pl.when(e == pl.num_programs(1) - 1)
    def _():
        out = h_ref[...] + acc_ref[...]
        if final_norm:
            out = _rms(out, fg_ref[...])
        o_ref[...] = out


def _peer(h, y, gm, u, v, fg, tb, eb, final_norm):
    t, d = h.shape
    ne = v.shape[0]
    return pl.pallas_call(
        functools.partial(_peer_kernel, final_norm=final_norm),
        grid=(t // tb, ne // eb),
        in_specs=[pl.BlockSpec((tb, d), lambda i, e: (i, 0)),
                  pl.BlockSpec((tb, d), lambda i, e: (i, 0)),
                  pl.BlockSpec((tb, eb), lambda i, e: (i, e)),
                  pl.BlockSpec((eb, d), lambda i, e: (e, 0)),
                  pl.BlockSpec((eb, d), lambda i, e: (e, 0)),
                  pl.BlockSpec((1, d), lambda i, e: (0, 0))],
        out_specs=pl.BlockSpec((tb, d), lambda i, e: (i, 0)),
        out_shape=jax.ShapeDtypeStruct((t, d), jnp.float32),
        scratch_shapes=[pltpu.VMEM((tb, d), jnp.float32)],
        compiler_params=_cparams("parallel", "arbitrary"),
        name="peer_experts",
    )(h, y, gm, u, v, fg)


def _seq_block(lp):
    return 3 * BLOCK_Q if lp % (3 * BLOCK_Q) == 0 else BLOCK_Q


def _forward(x, meta_tokens, mix_norm_g, ffn_norm_g, final_norm_g, ab_w_in, ab_conv_w,
             ab_w_out, cf_w_pw1, cf_b_pw1, cf_conv_w, cf_conv_b, cf_ln_g, cf_ln_b,
             cf_w_pw2, cf_b_pw2, peer_w_q, peer_sub_keys, peer_u, peer_v,
             peer_tb, peer_eb):
    bsz, s, d = x.shape
    l = s + N_META
    lp = -(-l // BLOCK_Q) * BLOCK_Q
    t = bsz * lp
    depth = mix_norm_g.shape[0]
    seq_blk = _seq_block(lp)
    bf = jnp.bfloat16
    row = lambda a: a.reshape(1, -1)

    h = jnp.concatenate([
        jnp.broadcast_to(meta_tokens.astype(x.dtype)[None], (bsz, N_META, d)),
        x,
        jnp.zeros((bsz, lp - l, d), x.dtype)], axis=1).reshape(t, d)

    a_width = A_HEADS * A_HEAD_DIM
    tri = (jnp.arange(BLOCK_Q)[:, None] > jnp.arange(BLOCK_Q)[None, :]).astype(bf)
    mhalf = jnp.concatenate([tri, jnp.ones((BLOCK_Q, BLOCK_Q), bf)], axis=1)
    mcat = jnp.concatenate([mhalf, mhalf], axis=0)
    qscale = jnp.concatenate([jnp.full((a_width,), 1.0 / math.sqrt(A_HEAD_DIM), jnp.float32),
                              jnp.ones((ab_w_in.shape[2] - a_width,), jnp.float32)])

    for i in range(depth):
        j = i // 2
        if i % 2 == 0:
            w_in = (ab_w_in[j] * qscale[None, :]).astype(bf)
            qkv, bb = _inproj(h, row(mix_norm_g[i]), w_in, 3 * a_width, seq_blk)
            a_out = _attention(qkv, mcat, bsz, lp, BLOCK_Q)
            h = _about(a_out, bb, ab_conv_w[j], ab_w_out[j].astype(bf), h, lp, seq_blk)
        else:
            z = _glu(h, row(mix_norm_g[i]), cf_w_pw1[j].astype(bf), row(cf_b_pw1[j]), seq_blk)
            h = _cfout(z, cf_conv_w[j], row(cf_conv_b[j]), row(cf_ln_g[j]), row(cf_ln_b[j]),
                       cf_w_pw2[j].astype(bf), row(cf_b_pw2[j]), h, lp, seq_blk)
        gm, u_bf, v_bf, y_bf = _route(h, row(ffn_norm_g[i]), peer_w_q[i].astype(bf),
                                      peer_sub_keys[i].astype(bf), peer_u, peer_v, i)
        h = _peer(h, y_bf, gm, u_bf, v_bf, row(final_norm_g),
                  peer_tb, peer_eb, final_norm=(i == depth - 1))
    return h.reshape(bsz, lp, d)[:, N_META:N_META + s, :]


def kernel(x, meta_tokens, mix_norm_g, ffn_norm_g, final_norm_g, ab_w_in, ab_conv_w, ab_w_out, cf_w_pw1, cf_b_pw1, cf_conv_w, cf_conv_b, cf_ln_g, cf_ln_b, cf_w_pw2, cf_b_pw2, peer_w_q, peer_sub_keys, peer_u, peer_v):
    return _forward(x, meta_tokens, mix_norm_g, ffn_norm_g, final_norm_g, ab_w_in, ab_conv_w,
                    ab_w_out, cf_w_pw1, cf_b_pw1, cf_conv_w, cf_conv_b, cf_ln_g, cf_ln_b,
                    cf_w_pw2, cf_b_pw2, peer_w_q, peer_sub_keys, peer_u, peer_v,
                    peer_tb=768, peer_eb=1024)
```
